```python
import jax, jax.numpy as jnp
from jax import lax
import numpy as np

D_MODEL = 1024
BATCH = 8
SEQ = 4096
DEPTH = 4

EPS = 1e-6
GLA_HEADS = 4
GLA_DK = 64
GLA_DV = 128
GLA_GATE_RANK = 16
GLA_TAU = 16.0
GLA_CHUNK = 64
GLA_WIDTH = GLA_HEADS * GLA_DV
MLA_HEADS = 4
MLA_Q_RANK = 256
MLA_KV_RANK = 128
MLA_NOPE = 128
MLA_ROPE = 64
MLA_V = 128
MLA_QBLOCK = 128
MLA_WIDTH = MLA_HEADS * MLA_V
ROPE_THETA = 10000.0
SG_GROUPS = 8
SG_WIDTH = 2 * D_MODEL
SG_CHUNK = 128

N_EVEN = (DEPTH + 1) // 2
N_ODD = DEPTH // 2

EVEN_SPLITS = (GLA_HEADS * GLA_DK, GLA_HEADS * GLA_DK, GLA_WIDTH, GLA_GATE_RANK, GLA_GATE_RANK, GLA_WIDTH,
               MLA_Q_RANK, MLA_KV_RANK, MLA_ROPE, MLA_WIDTH)
EVEN_IN = sum(EVEN_SPLITS)
ODD_IN = 3 * SG_WIDTH

kernel_name = "hybrid_gla_mla_gmlp_encoder"


def _split(x, sizes):
    out, o = [], 0
    for s in sizes:
        out.append(x[..., o:o + s])
        o += s
    return out


def rms_norm(x, g):
    xf = x.astype(jnp.float32)
    y = xf * lax.rsqrt(jnp.mean(xf * xf, axis=-1, keepdims=True) + EPS)
    return (y * g.astype(jnp.float32)).astype(x.dtype)


def layer_norm(x, g, b):
    xf = x.astype(jnp.float32)
    mu = jnp.mean(xf, axis=-1, keepdims=True)
    var = jnp.mean(jnp.square(xf - mu), axis=-1, keepdims=True)
    y = (xf - mu) * lax.rsqrt(var + EPS)
    return (y * g.astype(jnp.float32) + b.astype(jnp.float32)).astype(x.dtype)


def apply_rope(x, cos, sin):
    x1, x2 = jnp.split(x, 2, axis=-1)
    return jnp.concatenate([x1 * cos - x2 * sin, x2 * cos + x1 * sin], axis=-1)


def gla_scan(q, k, v, g, strict):
    B, S, H, dk = q.shape
    dv = v.shape[-1]
    N = S // GLA_CHUNK

    def to_chunks(t):
        return t.reshape(B, N, GLA_CHUNK, H, t.shape[-1]).transpose(1, 0, 3, 2, 4)

    mask = jnp.tril(jnp.ones((GLA_CHUNK, GLA_CHUNK), bool), k=-1 if strict else 0)

    def step(state, inp):
        qc, kc, vc, gc = inp
        b = jnp.cumsum(gc, axis=2)
        diff = b[:, :, :, None, :] - b[:, :, None, :, :]
        decay = jnp.exp(jnp.where(mask[None, None, :, :, None], diff, -jnp.inf))
        att = jnp.einsum('bhid,bhjd,bhijd->bhij', qc, kc, decay)
        o = jnp.einsum('bhij,bhjv->bhiv', att, vc) + jnp.einsum('bhid,bhdv->bhiv', qc * jnp.exp(b), state)
        b_last = b[:, :, -1, :]
        state = jnp.exp(b_last)[..., None] * state + jnp.einsum(
            'bhjd,bhjv->bhdv', kc * jnp.exp(b_last[:, :, None, :] - b), vc)
        return state, o

    s0 = jnp.zeros((B, H, dk, dv), jnp.float32)
    _, outs = lax.scan(step, s0, (to_chunks(q), to_chunks(k), to_chunks(v), to_chunks(g)))
    return outs.transpose(1, 0, 3, 2, 4).reshape(B, S, H, dv)


def gla_branch(q, k, v, a_f, a_b, z, w2_f, b_f, w2_b, b_b, g_norm):
    B, S, _ = q.shape
    q = q.reshape(B, S, GLA_HEADS, GLA_DK) * (GLA_DK ** -0.5)
    k = k.reshape(B, S, GLA_HEADS, GLA_DK)
    v = v.reshape(B, S, GLA_HEADS, GLA_DV)
    g_f = (jax.nn.log_sigmoid((a_f @ w2_f + b_f).astype(jnp.float32)) / GLA_TAU).reshape(B, S, GLA_HEADS, GLA_DK)
    g_b = (jax.nn.log_sigmoid((a_b @ w2_b + b_b).astype(jnp.float32)) / GLA_TAU).reshape(B, S, GLA_HEADS, GLA_DK)
    o_f = gla_scan(q, k, v, g_f, False)
    flip = lambda t: jnp.flip(t, axis=1)
    o_b = flip(gla_scan(flip(q), flip(k), flip(v), flip(g_b), True))
    o = rms_norm((o_f + o_b).astype(z.dtype), g_norm).reshape(B, S, GLA_WIDTH)
    return o * jax.nn.silu(z)


def mla_branch(cq, ckv, kr, z, g_q, w_uq, g_kv, w_ukv, cos, sin):
    B, S, _ = cq.shape
    q = (rms_norm(cq, g_q) @ w_uq).reshape(B, S, MLA_HEADS, MLA_NOPE + MLA_ROPE)
    qn, qr = q[..., :MLA_NOPE], q[..., MLA_NOPE:]
    qr = apply_rope(qr, cos[:, :, None, :], sin[:, :, None, :])
    kv = (rms_norm(ckv, g_kv) @ w_ukv).reshape(B, S, MLA_HEADS, MLA_NOPE + MLA_V)
    kn, v = kv[..., :MLA_NOPE], kv[..., MLA_NOPE:]
    kr = apply_rope(kr, cos, sin)
    scale = (MLA_NOPE + MLA_ROPE) ** -0.5
    nb = S // MLA_QBLOCK
    qn_b = (qn * scale).reshape(B, nb, MLA_QBLOCK, MLA_HEADS, MLA_NOPE).transpose(1, 0, 2, 3, 4)
    qr_b = (qr * scale).reshape(B, nb, MLA_QBLOCK, MLA_HEADS, MLA_ROPE).transpose(1, 0, 2, 3, 4)

    def attend(blk):
        qn_i, qr_i = blk
        s = jnp.einsum('bqhd,bkhd->bhqk', qn_i, kn) + jnp.einsum('bqhr,bkr->bhqk', qr_i, kr)
        p = jax.nn.softmax(s.astype(jnp.float32), axis=-1).astype(v.dtype)
        return jnp.einsum('bhqk,bkhd->bqhd', p, v)

    o = lax.map(attend, (qn_b, qr_b))
    o = o.transpose(1, 0, 2, 3, 4).reshape(B, S, MLA_WIDTH)
    return o * jax.nn.silu(z)


def sgmlp_branch(p, g_ln, b_ln, w_s, b_s):
    B, S, _ = p.shape
    uv, z = p[..., :2 * SG_WIDTH], p[..., 2 * SG_WIDTH:]
    uv = jax.nn.gelu(uv, approximate=False)
    u, v = uv[..., :SG_WIDTH], uv[..., SG_WIDTH:]
    v = layer_norm(v, g_ln, b_ln)
    n = S // SG_CHUNK
    vc = v.reshape(B, n, SG_CHUNK, SG_GROUPS, SG_WIDTH // SG_GROUPS)
    mixed = jnp.einsum('gij,bnjgc->bnigc', w_s, vc) + b_s.T[:, :, None]
    return u * mixed.reshape(B, S, SG_WIDTH) * jax.nn.silu(z)


def setup_inputs(seed: int = 0) -> dict:
    key = jax.random.key(seed)
    ks = jax.random.split(key, 32)
    nrm = lambda k, shape, s: jax.random.normal(k, shape, jnp.float32) * s
    gain = lambda k, shape: 1.0 + nrm(k, shape, 0.02)
    D = D_MODEL
    x = nrm(ks[0], (BATCH, SEQ, D), 1.0)
    c = nrm(ks[1], (BATCH, D), 1.0)
    positions = jnp.arange(SEQ, dtype=jnp.int32)[None, :] + jax.random.randint(ks[2], (BATCH, 1), 0, 1024, jnp.int32)
    return {
        "x": x,
        "c": c,
        "positions": positions,
        "w_mod": nrm(ks[3], (DEPTH, D, 3 * D), 0.5 * D ** -0.5),
        "b_mod": nrm(ks[4], (DEPTH, 3 * D), 0.1),
        "g_pre": gain(ks[5], (DEPTH, D)),
        "g_post": gain(ks[6], (DEPTH, D)),
        "w_in_e": nrm(ks[7], (N_EVEN, D, EVEN_IN), D ** -0.5),
        "gla_w_gate_f": nrm(ks[8], (N_EVEN, GLA_GATE_RANK, GLA_HEADS * GLA_DK), GLA_GATE_RANK ** -0.5),
        "gla_b_gate_f": nrm(ks[9], (N_EVEN, GLA_HEADS * GLA_DK), 0.5),
        "gla_w_gate_b": nrm(ks[10], (N_EVEN, GLA_GATE_RANK, GLA_HEADS * GLA_DK), GLA_GATE_RANK ** -0.5),
        "gla_b_gate_b": nrm(ks[11], (N_EVEN, GLA_HEADS * GLA_DK), 0.5),
        "gla_g_norm": gain(ks[12], (N_EVEN, GLA_DV)),
        "mla_g_q": gain(ks[13], (N_EVEN, MLA_Q_RANK)),
        "mla_w_uq": nrm(ks[14], (N_EVEN, MLA_Q_RANK, MLA_HEADS * (MLA_NOPE + MLA_ROPE)), MLA_Q_RANK ** -0.5),
        "mla_g_kv": gain(ks[15], (N_EVEN, MLA_KV_RANK)),
        "mla_w_ukv": nrm(ks[16], (N_EVEN, MLA_KV_RANK, MLA_HEADS * (MLA_NOPE + MLA_V)), MLA_KV_RANK ** -0.5),
        "w_out_e": nrm(ks[17], (N_EVEN, GLA_WIDTH + MLA_WIDTH, D), (GLA_WIDTH + MLA_WIDTH) ** -0.5),
        "w_in_o": nrm(ks[18], (N_ODD, D, ODD_IN), D ** -0.5),
        "sg_g_norm": gain(ks[19], (N_ODD, SG_WIDTH)),
        "sg_b_norm": nrm(ks[20], (N_ODD, SG_WIDTH), 0.02),
        "sg_w_s": nrm(ks[21], (N_ODD, SG_GROUPS, SG_CHUNK, SG_CHUNK), SG_CHUNK ** -0.5),
        "sg_b_s": 1.0 + nrm(ks[22], (N_ODD, SG_GROUPS, SG_CHUNK), 0.1),
        "w_out_o": nrm(ks[23], (N_ODD, SG_WIDTH, D), SG_WIDTH ** -0.5),
    }


def reference(x, c, positions, w_mod, b_mod, g_pre, g_post, w_in_e, gla_w_gate_f, gla_b_gate_f,
              gla_w_gate_b, gla_b_gate_b, gla_g_norm, mla_g_q, mla_w_uq, mla_g_kv, mla_w_ukv, w_out_e,
              w_in_o, sg_g_norm, sg_b_norm, sg_w_s, sg_b_s, w_out_o):
    inv_freq = 1.0 / (ROPE_THETA ** (jnp.arange(0, MLA_ROPE, 2, dtype=jnp.float32) / MLA_ROPE))
    ang = positions.astype(jnp.float32)[..., None] * inv_freq
    cos, sin = jnp.cos(ang).astype(x.dtype), jnp.sin(ang).astype(x.dtype)
    c_act = jax.nn.silu(c)
    for l in range(DEPTH):
        mod = c_act @ w_mod[l] + b_mod[l]
        shift, scale, gate = jnp.split(mod[:, None, :], 3, axis=-1)
        h = rms_norm(x, g_pre[l]) * (1.0 + scale) + shift
        if l % 2 == 0:
            e = l // 2
            (gq, gk, gv, ga_f, ga_b, gz, cq, ckv, kr, mz) = _split(h @ w_in_e[e], EVEN_SPLITS)
            o_a = gla_branch(gq, gk, gv, ga_f, ga_b, gz, gla_w_gate_f[e], gla_b_gate_f[e],
                             gla_w_gate_b[e], gla_b_gate_b[e], gla_g_norm[e])
            o_b = mla_branch(cq, ckv, kr, mz, mla_g_q[e], mla_w_uq[e], mla_g_kv[e], mla_w_ukv[e], cos, sin)
            y = jnp.concatenate([o_a, o_b], axis=-1) @ w_out_e[e]
        else:
            o = l // 2
            y = sgmlp_branch(h @ w_in_o[o], sg_g_norm[o], sg_b_norm[o], sg_w_s[o], sg_b_s[o]) @ w_out_o[o]
        x = x + gate * rms_norm(y, g_post[l])
    return x
```

```python
import functools
import math

import numpy as np
import jax
import jax.numpy as jnp
from jax import lax
from jax.experimental import pallas as pl
from jax.experimental.pallas import tpu as pltpu

F32 = jnp.float32
BF16 = jnp.bfloat16

EPS = 1e-6
GLA_HEADS = 4
GLA_DK = 64
GLA_DV = 128
GLA_GATE_RANK = 16
GLA_TAU = 16.0
GLA_CHUNK = 64
GLA_QK = GLA_HEADS * GLA_DK
GLA_WIDTH = GLA_HEADS * GLA_DV
MLA_HEADS = 4
MLA_Q_RANK = 256
MLA_KV_RANK = 128
MLA_NOPE = 128
MLA_ROPE = 64
MLA_V = 128
MLA_WIDTH = MLA_HEADS * MLA_V
MLA_QK_PAD = 256
ROPE_THETA = 10000.0
SG_GROUPS = 8
SG_CHUNK = 128

LANE = 128
VMEM_LIMIT = 56 * 1024 * 1024

E_GQ, E_GK, E_GV, E_GZ, E_MZ, E_CQ, E_CKV, E_KR = 0, 256, 512, 1024, 1536, 2048, 2304, 2432
E_MAIN = 2560
E_GA = 128
E_TOTAL = E_MAIN + E_GA

GLA_LEVELS = (32, 16, 8, 4, 2, 1)


def _cparams(sem):
    return pltpu.CompilerParams(dimension_semantics=sem, vmem_limit_bytes=VMEM_LIMIT)


def _silu(x):
    return x * jax.nn.sigmoid(x)


def _gelu(x):
    return 0.5 * x * (1.0 + lax.erf(x * (2.0 ** -0.5)))


def _bdot(a, b):
    return jnp.dot(a, b, preferred_element_type=F32)


def _split_bf16(x):
    hi = x.astype(BF16)
    lo = (x - hi.astype(F32)).astype(BF16)
    return hi, lo


def _mod_kernel(c_ref, w_ref, b_ref, o_ref):
    ca = _silu(c_ref[...])
    o_ref[0] = jnp.dot(ca, w_ref[0], precision=lax.Precision.HIGHEST,
                       preferred_element_type=F32) + b_ref[0]


def _modulation(c, w_mod, b_mod):
    depth, d, n = w_mod.shape
    b = c.shape[0]
    tn = 1024
    return pl.pallas_call(
        _mod_kernel,
        grid=(depth, n // tn),
        in_specs=[
            pl.BlockSpec((b, d), lambda l, j: (0, 0)),
            pl.BlockSpec((1, d, tn), lambda l, j: (l, 0, j)),
            pl.BlockSpec((1, 1, tn), lambda l, j: (l, 0, j)),
        ],
        out_specs=pl.BlockSpec((1, b, tn), lambda l, j: (l, 0, j)),
        out_shape=jax.ShapeDtypeStruct((depth, b, n), F32),
        compiler_params=_cparams(("arbitrary", "arbitrary")),
        name="adaln_mod",
    )(c, w_mod, b_mod.reshape(depth, 1, n))


def _prenorm(x, mod_ref, g_ref, d):
    ms = jnp.mean(x * x, axis=-1, keepdims=True)
    shift = mod_ref[0, :, 0:d]
    scale = mod_ref[0, :, d:2 * d]
    return (x * lax.rsqrt(ms + EPS)) * g_ref[...] * (1.0 + scale) + shift


def _postnorm_residual(x, y, mod_ref, g_ref, d):
    ms = jnp.mean(y * y, axis=-1, keepdims=True)
    gate = mod_ref[0, :, 2 * d:3 * d]
    return x + gate * ((y * lax.rsqrt(ms + EPS)) * g_ref[...])


def _even_in_kernel(x_ref, mod_ref, g_ref, w_ref, o_ref, ga_ref, *, d):
    hb = _prenorm(x_ref[...], mod_ref, g_ref, d).astype(BF16)
    for c0 in range(0, E_MAIN, 512):
        o_ref[:, c0:c0 + 512] = _bdot(hb, w_ref[:, c0:c0 + 512]).astype(BF16)
    ga_ref[...] = _bdot(hb, w_ref[:, E_MAIN:E_TOTAL])


def _even_in_proj(x2, mod_l, g_pre, w_e, seq):
    t, d = x2.shape
    tm = min(512, seq)
    per_b = seq // tm
    return pl.pallas_call(
        functools.partial(_even_in_kernel, d=d),
        grid=(t // tm,),
        in_specs=[
            pl.BlockSpec((tm, d), lambda i: (i, 0)),
            pl.BlockSpec((1, 1, 3 * d), lambda i: (i // per_b, 0, 0)),
            pl.BlockSpec((1, d), lambda i: (0, 0)),
            pl.BlockSpec((d, E_TOTAL), lambda i: (0, 0)),
        ],
        out_specs=[
            pl.BlockSpec((tm, E_MAIN), lambda i: (i, 0)),
            pl.BlockSpec((tm, E_GA), lambda i: (i, 0)),
        ],
        out_shape=[
            jax.ShapeDtypeStruct((t, E_MAIN), BF16),
            jax.ShapeDtypeStruct((t, E_GA), F32),
        ],
        compiler_params=_cparams(("arbitrary",)),
        name="even_in_proj",
    )(x2, mod_l, g_pre, w_e)


def _gla_constants():
    c = GLA_CHUNK
    idx = np.arange(c)
    nl = len(GLA_LEVELS)
    wf = np.zeros((nl + 2, c, c), np.float32)
    wb = np.zeros((nl + 2, c, c), np.float32)
    mf = np.zeros((nl + 1, c, c), np.float32)
    mb = np.zeros((nl, c, c), np.float32)
    for li, s in enumerate(GLA_LEVELS):
        mid = (idx // (2 * s)) * (2 * s) + s
        upper = (idx % (2 * s)) >= s
        for t in range(c):
            if upper[t]:
                wf[li, t, mid[t]:t + 1] = 1.0
                wb[li, t, mid[t]:t] = 1.0
            else:
                wf[li, t, t + 1:mid[t]] = 1.0
                wb[li, t, t:mid[t]] = 1.0
        same = (idx[:, None] // (2 * s)) == (idx[None, :] // (2 * s))
        mf[li] = same & upper[:, None] & (~upper[None, :])
        mb[li] = same & (~upper[:, None]) & upper[None, :]
    mf[nl] = np.eye(c)
    for t in range(c):
        wf[nl, t, :t + 1] = 1.0
        wf[nl + 1, t, t + 1:] = 1.0
        wb[nl, t, t:] = 1.0
        wb[nl + 1, t, :t] = 1.0
    tile = lambda m: np.tile(m, (1, 1, GLA_HEADS))
    return (wf.reshape(-1, c), wb.reshape(-1, c), tile(mf), tile(mb))


def _gla_gate(ga, w2_ref, b2_ref):
    x = jnp.dot(ga, w2_ref[...], precision=lax.Precision.HIGHEST,
                preferred_element_type=F32) + b2_ref[...]
    return (jnp.minimum(x, 0.0) - jnp.log1p(jnp.exp(-jnp.abs(x)))) * (1.0 / GLA_TAU)


def _gla_chunk(q, k, v, g, wall_ref, mask_ref, kblk_ref, vblk_ref, sblk_ref, state_ref, ones_ref,
               diag_level):
    c = GLA_CHUNK
    nl = len(GLA_LEVELS)
    g_hi, g_lo = _split_bf16(g)
    wall = wall_ref[...]
    f = jnp.exp(_bdot(wall, g_hi) + _bdot(wall, g_lo))

    att = None
    n_levels = nl + 1 if diag_level else nl
    for li in range(n_levels):
        if li < nl:
            fl = f[li * c:(li + 1) * c]
            qs, ks = (q * fl).astype(BF16), (k * fl).astype(BF16)
        else:
            qs, ks = q.astype(BF16), k.astype(BF16)
        for h in range(GLA_HEADS):
            sl = slice(h * GLA_DK, (h + 1) * GLA_DK)
            kblk_ref[li, sl, sl] = ks[:, sl]
        r = lax.dot_general(qs, kblk_ref[li], (((1,), (1,)), ((), ())),
                            preferred_element_type=F32)
        r = r * mask_ref[li]
        att = r if att is None else att + r

    for h in range(GLA_HEADS):
        vblk_ref[h * c:(h + 1) * c, h * GLA_DV:(h + 1) * GLA_DV] = v[:, h * GLA_DV:(h + 1) * GLA_DV]
    o = _bdot(att.astype(BF16), vblk_ref[...])

    qb = (q * f[nl * c:(nl + 1) * c]).astype(BF16)
    o = o + _bdot(qb, sblk_ref[...])

    ke = (k * f[(nl + 1) * c:(nl + 2) * c]).astype(BF16)
    tn = (((0,), (0,)), ((), ()))
    upd = lax.dot_general(ke, v, tn, preferred_element_type=F32)
    tot = (lax.dot_general(g_hi, ones_ref[...], tn, preferred_element_type=F32)
           + lax.dot_general(g_lo, ones_ref[...], tn, preferred_element_type=F32))
    dec = jnp.exp(tot)
    for h in range(GLA_HEADS):
        rs = slice(h * GLA_DK, (h + 1) * GLA_DK)
        cs = slice(h * GLA_DV, (h + 1) * GLA_DV)
        s_new = state_ref[h] * dec[rs] + upd[rs, cs]
        state_ref[h] = s_new
        sblk_ref[rs, cs] = s_new.astype(BF16)
    return o


def _gla_kernel(qf_ref, kf_ref, vf_ref, gaf_ref, qb_ref, kb_ref, vb_ref, gab_ref,
                w2f_ref, b2f_ref, w2b_ref, b2b_ref, wallf_ref, wallb_ref, mf_ref, mb_ref, ones_ref,
                of_ref, ob_ref,
                gf_scr, gb_scr, kblk_f, kblk_b, vblk_f, vblk_b, sblk_f, sblk_b, st_f, st_b,
                *, n_chunks):
    @pl.when(pl.program_id(1) == 0)
    def _():
        for ref in (kblk_f, kblk_b, vblk_f, vblk_b, sblk_f, sblk_b, st_f, st_b):
            ref[...] = jnp.zeros(ref.shape, ref.dtype)

    gf_scr[...] = _gla_gate(gaf_ref[...], w2f_ref, b2f_ref)
    gb_scr[...] = _gla_gate(gab_ref[...], w2b_ref, b2b_ref)
    qscale = GLA_DK ** -0.5

    def body(ci, carry):
        rf = pl.multiple_of(ci * GLA_CHUNK, GLA_CHUNK)
        rows = pl.ds(rf, GLA_CHUNK)
        o = _gla_chunk(qf_ref[rows, :].astype(F32) * qscale, kf_ref[rows, :].astype(F32),
                       vf_ref[rows, :], gf_scr[rows, :], wallf_ref, mf_ref, kblk_f, vblk_f, sblk_f,
                       st_f, ones_ref, True)
        of_ref[rows, :] = o.astype(of_ref.dtype)
        rb = pl.multiple_of((n_chunks - 1 - ci) * GLA_CHUNK, GLA_CHUNK)
        rows = pl.ds(rb, GLA_CHUNK)
        o = _gla_chunk(qb_ref[rows, :].astype(F32) * qscale, kb_ref[rows, :].astype(F32),
                       vb_ref[rows, :], gb_scr[rows, :], wallb_ref, mb_ref, kblk_b, vblk_b, sblk_b,
                       st_b, ones_ref, False)
        ob_ref[rows, :] = o.astype(ob_ref.dtype)
        return carry

    lax.fori_loop(0, n_chunks, body, 0)


def _gla(p_main, p_ga, w2f, b2f, w2b, b2b, batch, seq):
    t = p_main.shape[0]
    rb = min(512, seq)
    nblk = seq // rb
    n_chunks = rb // GLA_CHUNK
    wallf, wallb, mf, mb = _gla_constants()
    wallf, wallb = jnp.asarray(wallf, BF16), jnp.asarray(wallb, BF16)
    mf, mb = jnp.asarray(mf, F32), jnp.asarray(mb, F32)
    ones = jnp.ones((GLA_CHUNK, GLA_DV), BF16)
    nl = len(GLA_LEVELS)
    c = GLA_CHUNK

    fwd = lambda b, n: b * nblk + n
    bwd = lambda b, n: b * nblk + (nblk - 1 - n)

    def row_specs(rowf):
        return [
            pl.BlockSpec((rb, GLA_QK), lambda b, n: (rowf(b, n), E_GQ // GLA_QK)),
            pl.BlockSpec((rb, GLA_QK), lambda b, n: (rowf(b, n), E_GK // GLA_QK)),
            pl.BlockSpec((rb, GLA_WIDTH), lambda b, n: (rowf(b, n), E_GV // GLA_WIDTH)),
            pl.BlockSpec((rb, E_GA), lambda b, n: (rowf(b, n), 0)),
        ]

    full = lambda a: pl.BlockSpec(a.shape, lambda b, n: (0,) * a.ndim)
    consts = (w2f, b2f, w2b, b2b, wallf, wallb, mf, mb, ones)
    return pl.pallas_call(
        functools.partial(_gla_kernel, n_chunks=n_chunks),
        grid=(batch, nblk),
        in_specs=row_specs(fwd) + row_specs(bwd) + [full(a) for a in consts],
        out_specs=[
            pl.BlockSpec((rb, GLA_WIDTH), lambda b, n: (fwd(b, n), 0)),
            pl.BlockSpec((rb, GLA_WIDTH), lambda b, n: (bwd(b, n), 0)),
        ],
        out_shape=[jax.ShapeDtypeStruct((t, GLA_WIDTH), BF16)] * 2,
        scratch_shapes=[
            pltpu.VMEM((rb, GLA_QK), F32), pltpu.VMEM((rb, GLA_QK), F32),
            pltpu.VMEM((nl + 1, GLA_HEADS * c, GLA_QK), BF16),
            pltpu.VMEM((nl + 1, GLA_HEADS * c, GLA_QK), BF16),
            pltpu.VMEM((GLA_HEADS * c, GLA_WIDTH), BF16), pltpu.VMEM((GLA_HEADS * c, GLA_WIDTH), BF16),
            pltpu.VMEM((GLA_QK, GLA_WIDTH), BF16), pltpu.VMEM((GLA_QK, GLA_WIDTH), BF16),
            pltpu.VMEM((GLA_HEADS, GLA_DK, GLA_DV), F32), pltpu.VMEM((GLA_HEADS, GLA_DK, GLA_DV), F32),
        ],
        compiler_params=_cparams(("arbitrary", "arbitrary")),
        name="gla_bidir",
    )(p_main, p_main, p_main, p_ga, p_main, p_main, p_main, p_ga, *consts)


def _rms(x, g):
    return (x * lax.rsqrt(jnp.mean(x * x, axis=-1, keepdims=True) + EPS)) * g


def _mla_prep_kernel(cq_ref, ckv_ref, kr_ref, cs_ref, sn_ref, csn_ref, gq_ref, wq_ref, gkv_ref, wkv_ref,
                     q_ref, k_ref, v_ref):
    scale = (MLA_NOPE + MLA_ROPE) ** -0.5
    hw = MLA_HEADS * MLA_NOPE
    hr = MLA_HEADS * MLA_ROPE
    tm = cq_ref.shape[0]
    qa = _bdot(_rms(cq_ref[...].astype(F32), gq_ref[...]).astype(BF16), wq_ref[...])
    cs2 = jnp.concatenate([cs_ref[...]] * (hr // LANE), axis=-1)
    sn2 = jnp.concatenate([sn_ref[...]] * (hr // LANE), axis=-1)
    qr = (qa[:, hw:hw + hr] * cs2 + qa[:, hw + hr:hw + 2 * hr] * sn2) * scale
    kva = _bdot(_rms(ckv_ref[...].astype(F32), gkv_ref[...]).astype(BF16), wkv_ref[...])
    kt = kr_ref[...].astype(F32) * csn_ref[...]
    kr = kt[:, :MLA_ROPE] + kt[:, MLA_ROPE:]
    pad = jnp.zeros((tm, MLA_QK_PAD - MLA_NOPE - MLA_ROPE), BF16)
    for h in range(MLA_HEADS):
        q_ref[0, h, :, 0:MLA_NOPE] = (qa[:, h * MLA_NOPE:(h + 1) * MLA_NOPE] * scale).astype(BF16)
        q_ref[0, h, :, MLA_NOPE:MLA_NOPE + MLA_ROPE] = qr[:, h * MLA_ROPE:(h + 1) * MLA_ROPE].astype(BF16)
        q_ref[0, h, :, MLA_NOPE + MLA_ROPE:] = pad
        base = h * (MLA_NOPE + MLA_V)
        k_ref[0, h, :, 0:MLA_NOPE] = kva[:, base:base + MLA_NOPE].astype(BF16)
        k_ref[0, h, :, MLA_NOPE:MLA_NOPE + MLA_ROPE] = kr.astype(BF16)
        k_ref[0, h, :, MLA_NOPE + MLA_ROPE:] = pad
        v_ref[0, h] = kva[:, base + MLA_NOPE:base + MLA_NOPE + MLA_V].astype(BF16)


def _mla_prep(p_main, cs, sn, csn, g_q, w_uq, g_kv, w_ukv, batch, seq):
    tm = min(512, seq)
    per_b = seq // tm
    row = lambda b, i: b * per_b + i
    full = lambda a: pl.BlockSpec(a.shape, lambda b, i: (0,) * a.ndim)
    return pl.pallas_call(
        _mla_prep_kernel,
        grid=(batch, per_b),
        in_specs=[
            pl.BlockSpec((tm, MLA_Q_RANK), lambda b, i: (row(b, i), E_CQ // MLA_Q_RANK)),
            pl.BlockSpec((tm, MLA_KV_RANK), lambda b, i: (row(b, i), E_CKV // MLA_KV_RANK)),
            pl.BlockSpec((tm, LANE), lambda b, i: (row(b, i), E_KR // LANE)),
            pl.BlockSpec((tm, LANE), lambda b, i: (row(b, i), 0)),
            pl.BlockSpec((tm, LANE), lambda b, i: (row(b, i), 0)),
            pl.BlockSpec((tm, LANE), lambda b, i: (row(b, i), 0)),
            full(g_q), full(w_uq), full(g_kv), full(w_ukv),
        ],
        out_specs=[
            pl.BlockSpec((1, MLA_HEADS, tm, MLA_QK_PAD), lambda b, i: (b, 0, i, 0)),
            pl.BlockSpec((1, MLA_HEADS, tm, MLA_QK_PAD), lambda b, i: (b, 0, i, 0)),
            pl.BlockSpec((1, MLA_HEADS, tm, MLA_V), lambda b, i: (b, 0, i, 0)),
        ],
        out_shape=[
            jax.ShapeDtypeStruct((batch, MLA_HEADS, seq, MLA_QK_PAD), BF16),
            jax.ShapeDtypeStruct((batch, MLA_HEADS, seq, MLA_QK_PAD), BF16),
            jax.ShapeDtypeStruct((batch, MLA_HEADS, seq, MLA_V), BF16),
        ],
        compiler_params=_cparams(("arbitrary", "arbitrary")),
        name="mla_prep",
    )(p_main, p_main, p_main, cs, sn, csn, g_q, w_uq, g_kv, w_ukv)


def _mla_attn_kernel(q_ref, k_ref, v_ref, z_ref, o_ref):
    s = lax.dot_general(q_ref[0, 0], k_ref[0, 0], (((1,), (1,)), ((), ())),
                        preferred_element_type=F32)
    m = jnp.max(s, axis=-1, keepdims=True)
    p = jnp.exp(s - m)
    l = jnp.sum(p, axis=-1, keepdims=True)
    o = _bdot(p.astype(BF16), v_ref[0, 0]) / l
    o_ref[...] = (o * _silu(z_ref[...].astype(F32))).astype(o_ref.dtype)


def _mla_attention(q_cat, k_cat, v, p_main, batch, seq):
    tq = min(256, seq)
    nq = seq // tq
    return pl.pallas_call(
        _mla_attn_kernel,
        grid=(batch, MLA_HEADS, nq),
        in_specs=[
            pl.BlockSpec((1, 1, tq, MLA_QK_PAD), lambda b, h, i: (b, h, i, 0)),
            pl.BlockSpec((1, 1, seq, MLA_QK_PAD), lambda b, h, i: (b, h, 0, 0)),
            pl.BlockSpec((1, 1, seq, MLA_V), lambda b, h, i: (b, h, 0, 0)),
            pl.BlockSpec((tq, MLA_V), lambda b, h, i: (b * nq + i, E_MZ // MLA_V + h)),
        ],
        out_specs=pl.BlockSpec((tq, MLA_V), lambda b, h, i: (b * nq + i, h)),
        out_shape=jax.ShapeDtypeStruct((batch * seq, MLA_WIDTH), BF16),
        compiler_params=_cparams(("arbitrary", "arbitrary", "arbitrary")),
        name="mla_attention",
    )(q_cat, k_cat, v, p_main)


def _even_out_kernel(x_ref, of_ref, ob_ref, gz_ref, om_ref, mod_ref, gn_ref, w_ref, gp_ref, o_ref, *, d):
    oa = of_ref[...].astype(F32) + ob_ref[...].astype(F32)
    gn = gn_ref[...]
    parts = [_rms(oa[:, h * GLA_DV:(h + 1) * GLA_DV], gn) for h in range(GLA_HEADS)]
    oa = jnp.concatenate(parts, axis=-1) * _silu(gz_ref[...].astype(F32))
    y = _bdot(oa.astype(BF16), w_ref[0:GLA_WIDTH, :]) + _bdot(om_ref[...], w_ref[GLA_WIDTH:, :])
    o_ref[...] = _postnorm_residual(x_ref[...], y, mod_ref, gp_ref, d)


def _even_out_proj(x2, o_f, o_b, p_main, o_mla, mod_l, g_norm, w_out, g_post, seq):
    t, d = x2.shape
    tm = min(512, seq)
    per_b = seq // tm
    return pl.pallas_call(
        functools.partial(_even_out_kernel, d=d),
        grid=(t // tm,),
        in_specs=[
            pl.BlockSpec((tm, d), lambda i: (i, 0)),
            pl.BlockSpec((tm, GLA_WIDTH), lambda i: (i, 0)),
            pl.BlockSpec((tm, GLA_WIDTH), lambda i: (i, 0)),
            pl.BlockSpec((tm, GLA_WIDTH), lambda i: (i, E_GZ // GLA_WIDTH)),
            pl.BlockSpec((tm, MLA_WIDTH), lambda i: (i, 0)),
            pl.BlockSpec((1, 1, 3 * d), lambda i: (i // per_b, 0, 0)),
            pl.BlockSpec((1, GLA_DV), lambda i: (0, 0)),
            pl.BlockSpec((GLA_WIDTH + MLA_WIDTH, d), lambda i: (0, 0)),
            pl.BlockSpec((1, d), lambda i: (0, 0)),
        ],
        out_specs=pl.BlockSpec((tm, d), lambda i: (i, 0)),
        out_shape=jax.ShapeDtypeStruct((t, d), F32),
        compiler_params=_cparams(("arbitrary",)),
        name="even_out_proj",
    )(x2, o_f, o_b, p_main, o_mla, mod_l, g_norm, w_out, g_post)


def _odd_kernel(x_ref, mod_ref, gpre_ref, win_ref, gln_ref, bln_ref, ws_ref, bs_ref, wout_ref, gpost_ref,
                o_ref, *, d, width):
    x = x_ref[...]
    tm = x.shape[0]
    hb = _prenorm(x, mod_ref, gpre_ref, d).astype(BF16)
    gw = width // SG_GROUPS

    v = _gelu(_bdot(hb, win_ref[:, width:2 * width]))
    mu = jnp.mean(v, axis=-1, keepdims=True)
    vc = v - mu
    var = jnp.mean(vc * vc, axis=-1, keepdims=True)
    vn = ((vc * lax.rsqrt(var + EPS)) * gln_ref[...] + bln_ref[...]).astype(BF16)

    rows = []
    for c in range(tm // SG_CHUNK):
        cols = []
        for g in range(SG_GROUPS):
            blk = vn[c * SG_CHUNK:(c + 1) * SG_CHUNK, g * gw:(g + 1) * gw]
            cols.append(_bdot(ws_ref[g], blk) + bs_ref[:, g:g + 1])
        rows.append(jnp.concatenate(cols, axis=-1))
    mixed = jnp.concatenate(rows, axis=0) if len(rows) > 1 else rows[0]

    u = _gelu(_bdot(hb, win_ref[:, 0:width]))
    z = _silu(_bdot(hb, win_ref[:, 2 * width:3 * width]))
    y = _bdot((u * mixed * z).astype(BF16), wout_ref[...])
    o_ref[...] = _postnorm_residual(x, y, mod_ref, gpost_ref, d)


def _odd_layer(x2, mod_l, g_pre, w_in, g_ln, b_ln, w_s, b_s_t, w_out, g_post, seq):
    t, d = x2.shape
    width = w_out.shape[0]
    tm = min(256, seq)
    per_b = seq // tm
    full = lambda a: pl.BlockSpec(a.shape, lambda i: (0,) * a.ndim)
    return pl.pallas_call(
        functools.partial(_odd_kernel, d=d, width=width),
        grid=(t // tm,),
        in_specs=[
            pl.BlockSpec((tm, d), lambda i: (i, 0)),
            pl.BlockSpec((1, 1, 3 * d), lambda i: (i // per_b, 0, 0)),
            full(g_pre), full(w_in), full(g_ln), full(b_ln), full(w_s), full(b_s_t), full(w_out),
            full(g_post),
        ],
        out_specs=pl.BlockSpec((tm, d), lambda i: (i, 0)),
        out_shape=jax.ShapeDtypeStruct((t, d), F32),
        compiler_params=_cparams(("arbitrary",)),
        name="odd_sgmlp",
    )(x2, mod_l, g_pre, w_in, g_ln, b_ln, w_s, b_s_t, w_out, g_post)


def _swap_halves(w):
    half = w.shape[-1] // 2
    return jnp.concatenate([w[..., half:], w[..., :half]], axis=-1)


def _even_in_weight(w):
    gq, gk, gv = w[:, 0:256], w[:, 256:512], w[:, 512:1024]
    ga_f, ga_b = w[:, 1024:1040], w[:, 1040:1056]
    gz, cq, ckv, kr, mz = w[:, 1056:1568], w[:, 1568:1824], w[:, 1824:1952], w[:, 1952:2016], w[:, 2016:2528]
    pad = jnp.zeros((w.shape[0], E_GA - 2 * GLA_GATE_RANK), w.dtype)
    return jnp.concatenate([gq, gk, gv, gz, mz, cq, ckv, kr, _swap_halves(kr), ga_f, ga_b, pad],
                           axis=-1).astype(BF16)


def _uq_weight(w):
    w3 = w.reshape(w.shape[0], MLA_HEADS, MLA_NOPE + MLA_ROPE)
    nope = w3[:, :, :MLA_NOPE].reshape(w.shape[0], -1)
    rope = w3[:, :, MLA_NOPE:]
    return jnp.concatenate([nope, rope.reshape(w.shape[0], -1), _swap_halves(rope).reshape(w.shape[0], -1)],
                           axis=-1).astype(BF16)


def _gate_weight(w, offset):
    out = jnp.zeros((E_GA, w.shape[1]), F32)
    return out.at[offset:offset + w.shape[0]].set(w)


def _rope_tables(positions):
    inv_freq = 1.0 / (ROPE_THETA ** (jnp.arange(0, MLA_ROPE, 2, dtype=F32) / MLA_ROPE))
    ang = positions.astype(F32)[..., None] * inv_freq
    cos, sin = jnp.cos(ang), jnp.sin(ang)
    t = positions.size
    cos, sin = cos.reshape(t, -1), sin.reshape(t, -1)
    cs = jnp.concatenate([cos, cos, cos, cos], axis=-1)
    sn = jnp.concatenate([-sin, sin, -sin, sin], axis=-1)
    csn = jnp.concatenate([cos, cos, -sin, sin], axis=-1)
    return cs, sn, csn


def kernel(x, c, positions, w_mod, b_mod, g_pre, g_post, w_in_e, gla_w_gate_f, gla_b_gate_f, gla_w_gate_b, gla_b_gate_b, gla_g_norm, mla_g_q, mla_w_uq, mla_g_kv, mla_w_ukv, w_out_e, w_in_o, sg_g_norm, sg_b_norm, sg_w_s, sg_b_s, w_out_o):
    batch, seq, d = x.shape
    depth = w_mod.shape[0]
    mod = _modulation(c, w_mod, b_mod)
    cs, sn, csn = _rope_tables(positions)
    x2 = x.reshape(batch * seq, d)
    row = lambda a: a.reshape(1, -1)
    for l in range(depth):
        mod_l = mod[l][:, None, :]
        if l % 2 == 0:
            e = l // 2
            p_main, p_ga = _even_in_proj(x2, mod_l, row(g_pre[l]), _even_in_weight(w_in_e[e]), seq)
            o_f, o_b = _gla(p_main, p_ga,
                            _gate_weight(gla_w_gate_f[e], 0), row(gla_b_gate_f[e]),
                            _gate_weight(gla_w_gate_b[e], GLA_GATE_RANK), row(gla_b_gate_b[e]),
                            batch, seq)
            q_cat, k_cat, v = _mla_prep(p_main, cs, sn, csn, row(mla_g_q[e]), _uq_weight(mla_w_uq[e]),
                                        row(mla_g_kv[e]), mla_w_ukv[e].astype(BF16), batch, seq)
            o_mla = _mla_attention(q_cat, k_cat, v, p_main, batch, seq)
            x2 = _even_out_proj(x2, o_f, o_b, p_main, o_mla, mod_l, row(gla_g_norm[e]),
                                w_out_e[e].astype(BF16), row(g_post[l]), seq)
        else:
            o = l // 2
            x2 = _odd_layer(x2, mod_l, row(g_pre[l]), w_in_o[o].astype(BF16), row(sg_g_norm[o]),
                            row(sg_b_norm[o]), sg_w_s[o].astype(BF16), sg_b_s[o].T, w_out_o[o].astype(BF16),
                            row(g_post[l]), seq)
    return x2.reshape(batch, seq, d)
```

```python
import functools
import math

import numpy as np
import jax
import jax.numpy as jnp
from jax import lax
from jax.experimental import pallas as pl
from jax.experimental.pallas import tpu as pltpu

F32 = jnp.float32
BF16 = jnp.bfloat16

EPS = 1e-6
GLA_HEADS = 4
GLA_DK = 64
GLA_DV = 128
GLA_GATE_RANK = 16
GLA_TAU = 16.0
GLA_CHUNK = 64
GLA_QK = GLA_HEADS * GLA_DK
GLA_WIDTH = GLA_HEADS * GLA_DV
MLA_HEADS = 4
MLA_Q_RANK = 256
MLA_KV_RANK = 128
MLA_NOPE = 128
MLA_ROPE = 64
MLA_V = 128
MLA_WIDTH = MLA_HEADS * MLA_V
MLA_QK_PAD = 256
MLA_TQ = 512
MLA_GROUPS = 4
MLA_TK = 512
MLA_ONES_ROWS = 16
ROPE_THETA = 10000.0
SG_GROUPS = 8
SG_CHUNK = 128

LANE = 128
VMEM_LIMIT = 56 * 1024 * 1024

E_GQ, E_GK, E_GV, E_GZ, E_MZ, E_CQ, E_CKV, E_KR = 0, 256, 512, 1024, 1536, 2048, 2304, 2432
E_MAIN = 2560
E_GA = 128
E_TOTAL = E_MAIN + E_GA

GLA_LEVELS = (32, 16, 8, 4, 2, 1)
GLA_UNROLL = 2


def _cparams(sem):
    return pltpu.CompilerParams(dimension_semantics=sem, vmem_limit_bytes=VMEM_LIMIT)


def _silu(x):
    return x * jax.nn.sigmoid(x)


def _gelu(x):
    return 0.5 * x * (1.0 + lax.erf(x * (2.0 ** -0.5)))


def _bdot(a, b):
    return jnp.dot(a, b, preferred_element_type=F32)


def _split_bf16(x):
    hi = x.astype(BF16)
    lo = (x - hi.astype(F32)).astype(BF16)
    return hi, lo


def _mod_kernel(c_ref, w_ref, b_ref, o_ref):
    ca = _silu(c_ref[...])
    o_ref[0] = jnp.dot(ca, w_ref[0], precision=lax.Precision.HIGHEST,
                       preferred_element_type=F32) + b_ref[0]


def _modulation(c, w_mod, b_mod):
    depth, d, n = w_mod.shape
    b = c.shape[0]
    tn = 1024
    return pl.pallas_call(
        _mod_kernel,
        grid=(depth, n // tn),
        in_specs=[
            pl.BlockSpec((b, d), lambda l, j: (0, 0)),
            pl.BlockSpec((1, d, tn), lambda l, j: (l, 0, j)),
            pl.BlockSpec((1, 1, tn), lambda l, j: (l, 0, j)),
        ],
        out_specs=pl.BlockSpec((1, b, tn), lambda l, j: (l, 0, j)),
        out_shape=jax.ShapeDtypeStruct((depth, b, n), F32),
        compiler_params=_cparams(("arbitrary", "arbitrary")),
        name="adaln_mod",
    )(c, w_mod, b_mod.reshape(depth, 1, n))


def _prenorm(x, mod_ref, g_ref, d):
    ms = jnp.mean(x * x, axis=-1, keepdims=True)
    shift = mod_ref[0, :, 0:d]
    scale = mod_ref[0, :, d:2 * d]
    return (x * lax.rsqrt(ms + EPS)) * g_ref[...] * (1.0 + scale) + shift


def _postnorm_residual(x, y, mod_ref, g_ref, d):
    ms = jnp.mean(y * y, axis=-1, keepdims=True)
    gate = mod_ref[0, :, 2 * d:3 * d]
    return x + gate * ((y * lax.rsqrt(ms + EPS)) * g_ref[...])


def _even_in_kernel(x_ref, mod_ref, g_ref, w_ref, o_ref, ga_ref, *, d):
    hb = _prenorm(x_ref[...], mod_ref, g_ref, d).astype(BF16)
    for c0 in range(0, E_MAIN, 512):
        o_ref[:, c0:c0 + 512] = _bdot(hb, w_ref[:, c0:c0 + 512]).astype(BF16)
    ga_ref[...] = _bdot(hb, w_ref[:, E_MAIN:E_TOTAL])


def _even_in_proj(x2, mod_l, g_pre, w_e, seq):
    t, d = x2.shape
    tm = min(512, seq)
    per_b = seq // tm
    return pl.pallas_call(
        functools.partial(_even_in_kernel, d=d),
        grid=(t // tm,),
        in_specs=[
            pl.BlockSpec((tm, d), lambda i: (i, 0)),
            pl.BlockSpec((1, 1, 3 * d), lambda i: (i // per_b, 0, 0)),
            pl.BlockSpec((1, d), lambda i: (0, 0)),
            pl.BlockSpec((d, E_TOTAL), lambda i: (0, 0)),
        ],
        out_specs=[
            pl.BlockSpec((tm, E_MAIN), lambda i: (i, 0)),
            pl.BlockSpec((tm, E_GA), lambda i: (i, 0)),
        ],
        out_shape=[
            jax.ShapeDtypeStruct((t, E_MAIN), BF16),
            jax.ShapeDtypeStruct((t, E_GA), F32),
        ],
        compiler_params=_cparams(("arbitrary",)),
        name="even_in_proj",
    )(x2, mod_l, g_pre, w_e)


def _gla_constants():
    c = GLA_CHUNK
    idx = np.arange(c)
    nl = len(GLA_LEVELS)
    wf = np.zeros((nl + 2, c, c), np.float32)
    wb = np.zeros((nl + 2, c, c), np.float32)
    mf = np.zeros((nl + 1, c, c), np.float32)
    mb = np.zeros((nl, c, c), np.float32)
    for li, s in enumerate(GLA_LEVELS):
        mid = (idx // (2 * s)) * (2 * s) + s
        upper = (idx % (2 * s)) >= s
        for t in range(c):
            if upper[t]:
                wf[li, t, mid[t]:t + 1] = 1.0
                wb[li, t, mid[t]:t] = 1.0
            else:
                wf[li, t, t + 1:mid[t]] = 1.0
                wb[li, t, t:mid[t]] = 1.0
        same = (idx[:, None] // (2 * s)) == (idx[None, :] // (2 * s))
        mf[li] = same & upper[:, None] & (~upper[None, :])
        mb[li] = same & (~upper[:, None]) & upper[None, :]
    mf[nl] = np.eye(c)
    for t in range(c):
        wf[nl, t, :t + 1] = 1.0
        wf[nl + 1, t, t + 1:] = 1.0
        wb[nl, t, t:] = 1.0
        wb[nl + 1, t, :t] = 1.0
    tile = lambda m: np.tile(m, (1, 1, GLA_HEADS))
    return (wf.reshape(-1, c), wb.reshape(-1, c), tile(mf), tile(mb))


def _gla_gate(ga, w2_ref, b2_ref):
    a_hi, a_lo = _split_bf16(ga)
    x = _bdot(jnp.concatenate([a_hi, a_lo, a_hi], axis=1), w2_ref[...]) + b2_ref[...]
    return (jnp.minimum(x, 0.0) - jnp.log(1.0 + jnp.exp(-jnp.abs(x)))) * (1.0 / GLA_TAU)


def _block_diag(blocks):
    n = len(blocks)
    zero = jnp.zeros_like(blocks[0])
    return jnp.concatenate(
        [jnp.concatenate([blocks[i] if i == j else zero for j in range(n)], axis=1) for i in range(n)],
        axis=0)


_TN = (((0,), (0,)), ((), ()))
_NT = (((1,), (1,)), ((), ()))


class _GlaDir:
    def __init__(self, qs, ks, vs, gs, wall_ref, mask_ref, head_mask, ones_ref, diag_level):
        self.q, self.k, self.v, self.g = qs, ks, vs, gs
        self.wall_ref, self.mask_ref, self.head_mask, self.ones_ref = wall_ref, mask_ref, head_mask, ones_ref
        self.n_levels = len(GLA_LEVELS) + (1 if diag_level else 0)
        self.n = len(qs)

    def exponents(self):
        self.g2 = [jnp.concatenate(_split_bf16(g), axis=0) for g in self.g]
        e = _bdot(self.wall_ref[...], jnp.concatenate(self.g2, axis=1))
        f = jnp.exp(e)
        self.f = [f[:, i * GLA_QK:(i + 1) * GLA_QK] for i in range(self.n)]

    def intra(self):
        c, nl = GLA_CHUNK, len(GLA_LEVELS)
        atts = []
        for q, k, f in zip(self.q, self.k, self.f):
            att = None
            for li in range(self.n_levels):
                if li < nl:
                    fl = f[li * c:(li + 1) * c]
                    qs, ks = (q * fl).astype(BF16), (k * fl).astype(BF16)
                else:
                    qs, ks = q.astype(BF16), k.astype(BF16)
                kblk = jnp.where(self.head_mask, jnp.concatenate([ks] * GLA_HEADS, axis=0),
                                 jnp.zeros((), BF16))
                r = lax.dot_general(qs, kblk, _NT, preferred_element_type=F32) * self.mask_ref[li]
                att = r if att is None else att + r
            atts.append(att.astype(BF16))
        self.o = [_bdot(att, _block_diag([v[:, h * GLA_DV:(h + 1) * GLA_DV] for h in range(GLA_HEADS)]))
                  for att, v in zip(atts, self.v)]

    def state_terms(self):
        c, nl = GLA_CHUNK, len(GLA_LEVELS)
        self.qb = [(q * f[nl * c:(nl + 1) * c]).astype(BF16) for q, f in zip(self.q, self.f)]
        self.upd, self.dec = [], []
        for k, v, f, g2 in zip(self.k, self.v, self.f, self.g2):
            ke = (k * f[(nl + 1) * c:(nl + 2) * c]).astype(BF16)
            self.upd.append(lax.dot_general(ke, v, _TN, preferred_element_type=F32))
            self.dec.append(jnp.exp(lax.dot_general(g2, self.ones_ref[...], _TN,
                                                    preferred_element_type=F32)))

    def scan(self, state_ref):
        states = [state_ref[h] for h in range(GLA_HEADS)]
        outs = []
        for i in range(self.n):
            sblk = _block_diag([s.astype(BF16) for s in states])
            outs.append(self.o[i] + _bdot(self.qb[i], sblk))
            states = [states[h] * self.dec[i][h * GLA_DK:(h + 1) * GLA_DK]
                      + self.upd[i][h * GLA_DK:(h + 1) * GLA_DK, h * GLA_DV:(h + 1) * GLA_DV]
                      for h in range(GLA_HEADS)]
        for h in range(GLA_HEADS):
            state_ref[h] = states[h]
        return outs


def _gla_kernel(qf_ref, kf_ref, vf_ref, gaf_ref, qb_ref, kb_ref, vb_ref, gab_ref,
                w2f_ref, b2f_ref, w2b_ref, b2b_ref, wallf_ref, wallb_ref, mf_ref, mb_ref, ones_ref,
                of_ref, ob_ref, gf_scr, gb_scr, st_f, st_b, *, n_chunks):
    @pl.when(pl.program_id(1) == 0)
    def _():
        st_f[...] = jnp.zeros(st_f.shape, st_f.dtype)
        st_b[...] = jnp.zeros(st_b.shape, st_b.dtype)

    gf_scr[...] = _gla_gate(gaf_ref[...], w2f_ref, b2f_ref)
    gb_scr[...] = _gla_gate(gab_ref[...], w2b_ref, b2b_ref)
    qscale = GLA_DK ** -0.5
    hc = GLA_HEADS * GLA_CHUNK
    head_mask = (lax.broadcasted_iota(jnp.int32, (hc, GLA_QK), 0) // GLA_CHUNK
                 == lax.broadcasted_iota(jnp.int32, (hc, GLA_QK), 1) // GLA_DK)

    def load(q_ref, k_ref, v_ref, g_scr, rows):
        return (q_ref[rows, :].astype(F32) * qscale, k_ref[rows, :].astype(F32), v_ref[rows, :], g_scr[rows, :])

    def body(t, carry):
        rows_f = [pl.ds(pl.multiple_of((t * GLA_UNROLL + u) * GLA_CHUNK, GLA_CHUNK), GLA_CHUNK)
                  for u in range(GLA_UNROLL)]
        rows_b = [pl.ds(pl.multiple_of((n_chunks - 1 - t * GLA_UNROLL - u) * GLA_CHUNK, GLA_CHUNK), GLA_CHUNK)
                  for u in range(GLA_UNROLL)]
        fw = _GlaDir(*zip(*[load(qf_ref, kf_ref, vf_ref, gf_scr, r) for r in rows_f]),
                     wallf_ref, mf_ref, head_mask, ones_ref, True)
        bw = _GlaDir(*zip(*[load(qb_ref, kb_ref, vb_ref, gb_scr, r) for r in rows_b]),
                     wallb_ref, mb_ref, head_mask, ones_ref, False)
        for stage in ("exponents", "intra", "state_terms"):
            getattr(fw, stage)()
            getattr(bw, stage)()
        for rows, o in zip(rows_f, fw.scan(st_f)):
            of_ref[rows, :] = o.astype(of_ref.dtype)
        for rows, o in zip(rows_b, bw.scan(st_b)):
            ob_ref[rows, :] = o.astype(ob_ref.dtype)
        return carry

    lax.fori_loop(0, n_chunks // GLA_UNROLL, body, 0)


def _gla(p_main, p_ga, w2f, b2f, w2b, b2b, batch, seq):
    t = p_main.shape[0]
    rb = min(512, seq)
    nblk = seq // rb
    n_chunks = rb // GLA_CHUNK
    wallf, wallb, mf, mb = _gla_constants()
    wallf = jnp.asarray(np.concatenate([wallf, wallf], axis=1), BF16)
    wallb = jnp.asarray(np.concatenate([wallb, wallb], axis=1), BF16)
    mf, mb = jnp.asarray(mf, F32), jnp.asarray(mb, F32)
    ones = jnp.ones((2 * GLA_CHUNK, GLA_DV), BF16)

    fwd = lambda b, n: b * nblk + n
    bwd = lambda b, n: b * nblk + (nblk - 1 - n)

    def row_specs(rowf):
        return [
            pl.BlockSpec((rb, GLA_QK), lambda b, n: (rowf(b, n), E_GQ // GLA_QK)),
            pl.BlockSpec((rb, GLA_QK), lambda b, n: (rowf(b, n), E_GK // GLA_QK)),
            pl.BlockSpec((rb, GLA_WIDTH), lambda b, n: (rowf(b, n), E_GV // GLA_WIDTH)),
            pl.BlockSpec((rb, E_GA), lambda b, n: (rowf(b, n), 0)),
        ]

    full = lambda a: pl.BlockSpec(a.shape, lambda b, n: (0,) * a.ndim)
    consts = (w2f, b2f, w2b, b2b, wallf, wallb, mf, mb, ones)
    return pl.pallas_call(
        functools.partial(_gla_kernel, n_chunks=n_chunks),
        grid=(batch, nblk),
        in_specs=row_specs(fwd) + row_specs(bwd) + [full(a) for a in consts],
        out_specs=[
            pl.BlockSpec((rb, GLA_WIDTH), lambda b, n: (fwd(b, n), 0)),
            pl.BlockSpec((rb, GLA_WIDTH), lambda b, n: (bwd(b, n), 0)),
        ],
        out_shape=[jax.ShapeDtypeStruct((t, GLA_WIDTH), BF16)] * 2,
        scratch_shapes=[
            pltpu.VMEM((rb, GLA_QK), F32), pltpu.VMEM((rb, GLA_QK), F32),
            pltpu.VMEM((GLA_HEADS, GLA_DK, GLA_DV), F32), pltpu.VMEM((GLA_HEADS, GLA_DK, GLA_DV), F32),
        ],
        compiler_params=_cparams(("arbitrary", "arbitrary")),
        name="gla_bidir",
    )(p_main, p_main, p_main, p_ga, p_main, p_main, p_main, p_ga, *consts)


def _rms(x, g):
    return (x * lax.rsqrt(jnp.mean(x * x, axis=-1, keepdims=True) + EPS)) * g


def _mla_prep_kernel(cq_ref, ckv_ref, kr_ref, cst_ref, snt_ref, csn_ref, gq_ref, wqt_ref, gkv_ref, wkn_ref,
                     wvt_ref, qt_ref, k_ref, vt_ref):
    scale = (MLA_NOPE + MLA_ROPE) ** -0.5 * math.log2(math.e)
    hw = MLA_HEADS * MLA_NOPE
    hr = MLA_HEADS * MLA_ROPE
    tm = cq_ref.shape[0]
    cqn =_rms(cq_ref[...].astype(F32), gq_ref[...]).astype(BF16)
    qt = lax.dot_general(wqt_ref[...], cqn, _NT, preferred_element_type=F32)
    cs2 = jnp.concatenate([cst_ref[...]] * MLA_HEADS, axis=0)
    sn2 = jnp.concatenate([snt_ref[...]] * MLA_HEADS, axis=0)
    qr = (qt[hw:hw + hr] * cs2 + qt[hw + hr:hw + 2 * hr] * sn2) * scale
    kvn = _rms(ckv_ref[...].astype(F32), gkv_ref[...]).astype(BF16)
    kn = _bdot(kvn, wkn_ref[...])
    vt = lax.dot_general(wvt_ref[...], kvn, _NT, preferred_element_type=F32)
    kt = kr_ref[...].astype(F32) * csn_ref[...]
    kr = (kt[:, :MLA_ROPE] + kt[:, MLA_ROPE:]).astype(BF16)
    npad = MLA_QK_PAD - MLA_NOPE - MLA_ROPE
    for h in range(MLA_HEADS):
        qt_ref[0, h, 0:MLA_NOPE, :] = (qt[h * MLA_NOPE:(h + 1) * MLA_NOPE] * scale).astype(BF16)
        qt_ref[0, h, MLA_NOPE:MLA_NOPE + MLA_ROPE, :] = qr[h * MLA_ROPE:(h + 1) * MLA_ROPE].astype(BF16)
        qt_ref[0, h, MLA_NOPE + MLA_ROPE:, :] = jnp.zeros((npad, tm), BF16)
        k_ref[0, h, :, 0:MLA_NOPE] = kn[:, h * MLA_NOPE:(h + 1) * MLA_NOPE].astype(BF16)
        k_ref[0, h, :, MLA_NOPE:MLA_NOPE + MLA_ROPE] = kr
        k_ref[0, h, :, MLA_NOPE + MLA_ROPE:] = jnp.zeros((tm, npad), BF16)
        vt_ref[0, h] = vt[h * MLA_V:(h + 1) * MLA_V].astype(BF16)


def _mla_prep(p_main, cst, snt, csn, g_q, w_uqt, g_kv, w_kn, w_vt, batch, seq):
    tm = min(512, seq)
    per_b = seq // tm
    row = lambda b, i: b * per_b + i
    full = lambda a: pl.BlockSpec(a.shape, lambda b, i: (0,) * a.ndim)
    return pl.pallas_call(
        _mla_prep_kernel,
        grid=(batch, per_b),
        in_specs=[
            pl.BlockSpec((tm, MLA_Q_RANK), lambda b, i: (row(b, i), E_CQ // MLA_Q_RANK)),
            pl.BlockSpec((tm, MLA_KV_RANK), lambda b, i: (row(b, i), E_CKV // MLA_KV_RANK)),
            pl.BlockSpec((tm, LANE), lambda b, i: (row(b, i), E_KR // LANE)),
            pl.BlockSpec((MLA_ROPE, tm), lambda b, i: (0, row(b, i))),
            pl.BlockSpec((MLA_ROPE, tm), lambda b, i: (0, row(b, i))),
            pl.BlockSpec((tm, LANE), lambda b, i: (row(b, i), 0)),
            full(g_q), full(w_uqt), full(g_kv), full(w_kn), full(w_vt),
        ],
        out_specs=[
            pl.BlockSpec((1, MLA_HEADS, MLA_QK_PAD, tm), lambda b, i: (b, 0, 0, i)),
            pl.BlockSpec((1, MLA_HEADS, tm, MLA_QK_PAD), lambda b, i: (b, 0, i, 0)),
            pl.BlockSpec((1, MLA_HEADS, MLA_V, tm), lambda b, i: (b, 0, 0, i)),
        ],
        out_shape=[
            jax.ShapeDtypeStruct((batch, MLA_HEADS, MLA_QK_PAD, seq), BF16),
            jax.ShapeDtypeStruct((batch, MLA_HEADS, seq, MLA_QK_PAD), BF16),
            jax.ShapeDtypeStruct((batch, MLA_HEADS, MLA_V, seq), BF16),
        ],
        compiler_params=_cparams(("arbitrary", "arbitrary")),
        name="mla_prep",
    )(p_main, p_main, p_main, cst, snt, csn, g_q, w_uqt, g_kv, w_kn, w_vt)


def _mla_attn_kernel(qt_ref, k_ref, vt_ref, z_ref, o_ref):
    seq = k_ref.shape[2]
    tk = min(MLA_TK, seq)
    nk = seq // tk
    tq = qt_ref.shape[3]
    sub = min(MLA_TQ, tq)
    ones = jnp.ones((MLA_ONES_ROWS, tk), BF16)

    def score_block(c, j):
        qt = qt_ref[0, 0, :, c * sub:(c + 1) * sub]
        s = _bdot(k_ref[0, 0, j * tk:(j + 1) * tk, :], qt)
        return s, jnp.max(s, axis=0, keepdims=True)

    def value_block(s, m, j):
        p = jnp.exp2(s - m).astype(BF16)
        va = jnp.concatenate([vt_ref[0, 0, :, j * tk:(j + 1) * tk], ones], axis=0)
        return _bdot(va, p)

    groups = []
    for c in range(tq // sub):
        blocks = [score_block(c, j) for j in range(nk)]
        m = functools.reduce(jnp.maximum, [mj for _, mj in blocks])
        groups.append(([s for s, _ in blocks], m))
    for c, (s_blocks, m) in enumerate(groups):
        acc = functools.reduce(jnp.add, [value_block(s, m, j) for j, s in enumerate(s_blocks)])
        rows = slice(c * sub, (c + 1) * sub)
        ot = acc[:MLA_V] / acc[MLA_V:MLA_V + 1]
        o_ref[rows, :] = (ot.T * _silu(z_ref[rows, :].astype(F32))).astype(o_ref.dtype)


def _mla_attention(q_t, k_cat, v_t, p_main, batch, seq):
    tq = min(MLA_TQ * MLA_GROUPS, seq)
    nq = seq // tq
    return pl.pallas_call(
        _mla_attn_kernel,
        grid=(batch, MLA_HEADS, nq),
        in_specs=[
            pl.BlockSpec((1, 1, MLA_QK_PAD, tq), lambda b, h, i: (b, h, 0, i)),
            pl.BlockSpec((1, 1, seq, MLA_QK_PAD), lambda b, h, i: (b, h, 0, 0)),
            pl.BlockSpec((1, 1, MLA_V, seq), lambda b, h, i: (b, h, 0, 0)),
            pl.BlockSpec((tq, MLA_V), lambda b, h, i: (b * nq + i, E_MZ // MLA_V + h)),
        ],
        out_specs=pl.BlockSpec((tq, MLA_V), lambda b, h, i: (b * nq + i, h)),
        out_shape=jax.ShapeDtypeStruct((batch * seq, MLA_WIDTH), BF16),
        compiler_params=_cparams(("arbitrary", "arbitrary", "arbitrary")),
        name="mla_attention",
    )(q_t, k_cat, v_t, p_main)


def _even_out_kernel(x_ref, of_ref, ob_ref, gz_ref, om_ref, mod_ref, gn_ref, w_ref, gp_ref, o_ref, *, d):
    oa = of_ref[...].astype(F32) + ob_ref[...].astype(F32)
    gn = gn_ref[...]
    parts = [_rms(oa[:, h * GLA_DV:(h + 1) * GLA_DV], gn) for h in range(GLA_HEADS)]
    oa = jnp.concatenate(parts, axis=-1) * _silu(gz_ref[...].astype(F32))
    y = _bdot(oa.astype(BF16), w_ref[0:GLA_WIDTH, :]) + _bdot(om_ref[...], w_ref[GLA_WIDTH:, :])
    o_ref[...] = _postnorm_residual(x_ref[...], y, mod_ref, gp_ref, d)


def _even_out_proj(x2, o_f, o_b, p_main, o_mla, mod_l, g_norm, w_out, g_post, seq):
    t, d = x2.shape
    tm = min(512, seq)
    per_b = seq // tm
    return pl.pallas_call(
        functools.partial(_even_out_kernel, d=d),
        grid=(t // tm,),
        in_specs=[
            pl.BlockSpec((tm, d), lambda i: (i, 0)),
            pl.BlockSpec((tm, GLA_WIDTH), lambda i: (i, 0)),
            pl.BlockSpec((tm, GLA_WIDTH), lambda i: (i, 0)),
            pl.BlockSpec((tm, GLA_WIDTH), lambda i: (i, E_GZ // GLA_WIDTH)),
            pl.BlockSpec((tm, MLA_WIDTH), lambda i: (i, 0)),
            pl.BlockSpec((1, 1, 3 * d), lambda i: (i // per_b, 0, 0)),
            pl.BlockSpec((1, GLA_DV), lambda i: (0, 0)),
            pl.BlockSpec((GLA_WIDTH + MLA_WIDTH, d), lambda i: (0, 0)),
            pl.BlockSpec((1, d), lambda i: (0, 0)),
        ],
        out_specs=pl.BlockSpec((tm, d), lambda i: (i, 0)),
        out_shape=jax.ShapeDtypeStruct((t, d), F32),
        compiler_params=_cparams(("arbitrary",)),
        name="even_out_proj",
    )(x2, o_f, o_b, p_main, o_mla, mod_l, g_norm, w_out, g_post)


def _odd_kernel(x_ref, mod_ref, gpre_ref, win_ref, gln_ref, bln_ref, ws_ref, bs_ref, wout_ref, gpost_ref,
                o_ref, *, d, width):
    x = x_ref[...]
    tm = x.shape[0]
    hb = _prenorm(x, mod_ref, gpre_ref, d).astype(BF16)
    gw = width // SG_GROUPS

    v = _gelu(_bdot(hb, win_ref[:, width:2 * width]))
    mu = jnp.mean(v, axis=-1, keepdims=True)
    vc = v - mu
    var = jnp.mean(vc * vc, axis=-1, keepdims=True)
    vn = ((vc * lax.rsqrt(var + EPS)) * gln_ref[...] + bln_ref[...]).astype(BF16)

    rows = []
    for c in range(tm // SG_CHUNK):
        cols = []
        for g in range(SG_GROUPS):
            blk = vn[c * SG_CHUNK:(c + 1) * SG_CHUNK, g * gw:(g + 1) * gw]
            cols.append(_bdot(ws_ref[g], blk) + bs_ref[:, g:g + 1])
        rows.append(jnp.concatenate(cols, axis=-1))
    mixed = jnp.concatenate(rows, axis=0) if len(rows) > 1 else rows[0]

    u = _gelu(_bdot(hb, win_ref[:, 0:width]))
    z = _silu(_bdot(hb, win_ref[:, 2 * width:3 * width]))
    y = _bdot((u * mixed * z).astype(BF16), wout_ref[...])
    o_ref[...] = _postnorm_residual(x, y, mod_ref, gpost_ref, d)


def _odd_layer(x2, mod_l, g_pre, w_in, g_ln, b_ln, w_s, b_s_t, w_out, g_post, seq):
    t, d = x2.shape
    width = w_out.shape[0]
    tm = min(256, seq)
    per_b = seq // tm
    full = lambda a: pl.BlockSpec(a.shape, lambda i: (0,) * a.ndim)
    return pl.pallas_call(
        functools.partial(_odd_kernel, d=d, width=width),
        grid=(t // tm,),
        in_specs=[
            pl.BlockSpec((tm, d), lambda i: (i, 0)),
            pl.BlockSpec((1, 1, 3 * d), lambda i: (i // per_b, 0, 0)),
            full(g_pre), full(w_in), full(g_ln), full(b_ln), full(w_s), full(b_s_t), full(w_out),
            full(g_post),
        ],
        out_specs=pl.BlockSpec((tm, d), lambda i: (i, 0)),
        out_shape=jax.ShapeDtypeStruct((t, d), F32),
        compiler_params=_cparams(("arbitrary",)),
        name="odd_sgmlp",
    )(x2, mod_l, g_pre, w_in, g_ln, b_ln, w_s, b_s_t, w_out, g_post)


def _swap_halves(w):
    half = w.shape[-1] // 2
    return jnp.concatenate([w[..., half:], w[..., :half]], axis=-1)


def _even_in_weight(w):
    gq, gk, gv = w[:, 0:256], w[:, 256:512], w[:, 512:1024]
    ga_f, ga_b = w[:, 1024:1040], w[:, 1040:1056]
    gz, cq, ckv, kr, mz = w[:, 1056:1568], w[:, 1568:1824], w[:, 1824:1952], w[:, 1952:2016], w[:, 2016:2528]
    pad = jnp.zeros((w.shape[0], E_GA - 2 * GLA_GATE_RANK), w.dtype)
    return jnp.concatenate([gq, gk, gv, gz, mz, cq, ckv, kr, _swap_halves(kr), ga_f, ga_b, pad],
                           axis=-1).astype(BF16)


def _uq_weight(w):
    w3 = w.reshape(w.shape[0], MLA_HEADS, MLA_NOPE + MLA_ROPE)
    nope = w3[:, :, :MLA_NOPE].reshape(w.shape[0], -1)
    rope = w3[:, :, MLA_NOPE:]
    return jnp.concatenate([nope, rope.reshape(w.shape[0], -1), _swap_halves(rope).reshape(w.shape[0], -1)],
                           axis=-1).astype(BF16).T


def _ukv_weights(w):
    w3 = w.reshape(w.shape[0], MLA_HEADS, MLA_NOPE + MLA_V)
    w_kn = w3[:, :, :MLA_NOPE].reshape(w.shape[0], -1)
    w_v = w3[:, :, MLA_NOPE:].reshape(w.shape[0], -1)
    return w_kn.astype(BF16), w_v.astype(BF16).T


def _gate_weight(w, offset):
    out = jnp.zeros((E_GA, w.shape[1]), F32).at[offset:offset + w.shape[0]].set(w)
    hi, lo = _split_bf16(out)
    return jnp.concatenate([hi, hi, lo], axis=0)


def _rope_tables(positions):
    inv_freq = 1.0 / (ROPE_THETA ** (jnp.arange(0, MLA_ROPE, 2, dtype=F32) / MLA_ROPE))
    ang = positions.astype(F32)[..., None] * inv_freq
    cos, sin = jnp.cos(ang), jnp.sin(ang)
    t = positions.size
    cos, sin = cos.reshape(t, -1), sin.reshape(t, -1)
    cst = jnp.concatenate([cos, cos], axis=-1).T
    snt = jnp.concatenate([-sin, sin], axis=-1).T
    csn = jnp.concatenate([cos, cos, -sin, sin], axis=-1)
    return cst, snt, csn


def kernel(x, c, positions, w_mod, b_mod, g_pre, g_post, w_in_e, gla_w_gate_f, gla_b_gate_f, gla_w_gate_b, gla_b_gate_b, gla_g_norm, mla_g_q, mla_w_uq, mla_g_kv, mla_w_ukv, w_out_e, w_in_o, sg_g_norm, sg_b_norm, sg_w_s, sg_b_s, w_out_o):
    batch, seq, d = x.shape
    depth = w_mod.shape[0]
    mod = _modulation(c, w_mod, b_mod)
    cst, snt, csn = _rope_tables(positions)
    x2 = x.reshape(batch * seq, d)
    row = lambda a: a.reshape(1, -1)
    for l in range(depth):
        mod_l = mod[l][:, None, :]
        if l % 2 == 0:
            e = l // 2
            p_main, p_ga = _even_in_proj(x2, mod_l, row(g_pre[l]), _even_in_weight(w_in_e[e]), seq)
            o_f, o_b = _gla(p_main, p_ga,
                            _gate_weight(gla_w_gate_f[e], 0), row(gla_b_gate_f[e]),
                            _gate_weight(gla_w_gate_b[e], GLA_GATE_RANK), row(gla_b_gate_b[e]),
                            batch, seq)
            w_kn, w_vt = _ukv_weights(mla_w_ukv[e])
            q_t, k_cat, v_t = _mla_prep(p_main, cst, snt, csn, row(mla_g_q[e]), _uq_weight(mla_w_uq[e]),
                                        row(mla_g_kv[e]), w_kn, w_vt, batch, seq)
            o_mla = _mla_attention(q_t, k_cat, v_t, p_main, batch, seq)
            x2 = _even_out_proj(x2, o_f, o_b, p_main, o_mla, mod_l, row(gla_g_norm[e]),
                                w_out_e[e].astype(BF16), row(g_post[l]), seq)
        else:
            o = l // 2
            x2 = _odd_layer(x2, mod_l, row(g_pre[l]), w_in_o[o].astype(BF16), row(sg_g_norm[o]),
                            row(sg_b_norm[o]), sg_w_s[o].astype(BF16), sg_b_s[o].T, w_out_o[o].astype(BF16),
                            row(g_post[l]), seq)
    return x2.reshape(batch, seq, d)
```

```python
import functools
import math

import numpy as np
import jax
import jax.numpy as jnp
from jax import lax
from jax.experimental import pallas as pl
from jax.experimental.pallas import tpu as pltpu

F32 = jnp.float32
BF16 = jnp.bfloat16

EPS = 1e-6
GLA_HEADS = 4
GLA_DK = 64
GLA_DV = 128
GLA_GATE_RANK = 16
GLA_TAU = 16.0
GLA_CHUNK = 64
GLA_QK = GLA_HEADS * GLA_DK
GLA_WIDTH = GLA_HEADS * GLA_DV
MLA_HEADS = 4
MLA_Q_RANK = 256
MLA_KV_RANK = 128
MLA_NOPE = 128
MLA_ROPE = 64
MLA_V = 128
MLA_WIDTH = MLA_HEADS * MLA_V
MLA_QK_PAD = 256
MLA_TQ = 512
MLA_GROUPS = 8
MLA_TK = 512
MLA_ONES_ROWS = 16
ROPE_THETA = 10000.0
SG_GROUPS = 8
SG_CHUNK = 128
ODD_COLS = 512
ODD_ROWS = 512

LANE = 128
VMEM_LIMIT = 56 * 1024 * 1024

E_GQ, E_GK, E_GV, E_GZ, E_MZ, E_CQ, E_CKV, E_KR = 0, 256, 512, 1024, 1536, 2048, 2304, 2432
E_MAIN = 2560
E_GA = 128
E_TOTAL = E_MAIN + E_GA

GLA_LEVELS = (32, 16, 8, 4, 2, 1)
GLA_UNROLL = 2


def _cparams(sem, flags=None):
    return pltpu.CompilerParams(dimension_semantics=sem, vmem_limit_bytes=VMEM_LIMIT, flags=flags)


def _silu(x):
    return x * jax.nn.sigmoid(x)


def _gelu(x):
    return 0.5 * x * (1.0 + lax.erf(x * (2.0 ** -0.5)))


def _bdot(a, b):
    return jnp.dot(a, b, preferred_element_type=F32)


def _split_bf16(x):
    hi = x.astype(BF16)
    lo = (x - hi.astype(F32)).astype(BF16)
    return hi, lo


def _mod_kernel(c_ref, w_ref, b_ref, o_ref):
    ca = _silu(c_ref[...])
    o_ref[0] = jnp.dot(ca, w_ref[0], precision=lax.Precision.HIGHEST,
                       preferred_element_type=F32) + b_ref[0]


def _modulation(c, w_mod, b_mod):
    depth, d, n = w_mod.shape
    b = c.shape[0]
    tn = 1024
    return pl.pallas_call(
        _mod_kernel,
        grid=(depth, n // tn),
        in_specs=[
            pl.BlockSpec((b, d), lambda l, j: (0, 0)),
            pl.BlockSpec((1, d, tn), lambda l, j: (l, 0, j)),
            pl.BlockSpec((1, 1, tn), lambda l, j: (l, 0, j)),
        ],
        out_specs=pl.BlockSpec((1, b, tn), lambda l, j: (l, 0, j)),
        out_shape=jax.ShapeDtypeStruct((depth, b, n), F32),
        compiler_params=_cparams(("arbitrary", "arbitrary")),
        name="adaln_mod",
    )(c, w_mod, b_mod.reshape(depth, 1, n))


def _prenorm(x, mod_ref, g_ref, d):
    ms = jnp.mean(x * x, axis=-1, keepdims=True)
    shift = mod_ref[0, :, 0:d]
    scale = mod_ref[0, :, d:2 * d]
    return (x * lax.rsqrt(ms + EPS)) * g_ref[...] * (1.0 + scale) + shift


def _postnorm_residual(x, y, mod_ref, g_ref, d):
    ms = jnp.mean(y * y, axis=-1, keepdims=True)
    gate = mod_ref[0, :, 2 * d:3 * d]
    return x + gate * ((y * lax.rsqrt(ms + EPS)) * g_ref[...])


def _even_in_kernel(x_ref, mod_ref, g_ref, w_ref, o_ref, ga_ref, *, d):
    hb = _prenorm(x_ref[...], mod_ref, g_ref, d).astype(BF16)
    for c0 in range(0, E_MAIN, 512):
        o_ref[:, c0:c0 + 512] = _bdot(hb, w_ref[:, c0:c0 + 512]).astype(BF16)
    ga_ref[...] = _bdot(hb, w_ref[:, E_MAIN:E_TOTAL])


def _even_in_proj(x2, mod_l, g_pre, w_e, seq):
    t, d = x2.shape
    tm = min(512, seq)
    per_b = seq // tm
    return pl.pallas_call(
        functools.partial(_even_in_kernel, d=d),
        grid=(t // tm,),
        in_specs=[
            pl.BlockSpec((tm, d), lambda i: (i, 0)),
            pl.BlockSpec((1, 1, 3 * d), lambda i: (i // per_b, 0, 0)),
            pl.BlockSpec((1, d), lambda i: (0, 0)),
            pl.BlockSpec((d, E_TOTAL), lambda i: (0, 0)),
        ],
        out_specs=[
            pl.BlockSpec((tm, E_MAIN), lambda i: (i, 0)),
            pl.BlockSpec((tm, E_GA), lambda i: (i, 0)),
        ],
        out_shape=[
            jax.ShapeDtypeStruct((t, E_MAIN), BF16),
            jax.ShapeDtypeStruct((t, E_GA), F32),
        ],
        compiler_params=_cparams(("arbitrary",)),
        name="even_in_proj",
    )(x2, mod_l, g_pre, w_e)


def _gla_constants():
    c = GLA_CHUNK
    idx = np.arange(c)
    nl = len(GLA_LEVELS)
    wf = np.zeros((nl + 2, c, c), np.float32)
    wb = np.zeros((nl + 2, c, c), np.float32)
    mf = np.zeros((nl + 1, c, c), np.float32)
    mb = np.zeros((nl, c, c), np.float32)
    for li, s in enumerate(GLA_LEVELS):
        mid = (idx // (2 * s)) * (2 * s) + s
        upper = (idx % (2 * s)) >= s
        for t in range(c):
            if upper[t]:
                wf[li, t, mid[t]:t + 1] = 1.0
                wb[li, t, mid[t]:t] = 1.0
            else:
                wf[li, t, t + 1:mid[t]] = 1.0
                wb[li, t, t:mid[t]] = 1.0
        same = (idx[:, None] // (2 * s)) == (idx[None, :] // (2 * s))
        mf[li] = same & upper[:, None] & (~upper[None, :])
        mb[li] = same & (~upper[:, None]) & upper[None, :]
    mf[nl] = np.eye(c)
    for t in range(c):
        wf[nl, t, :t + 1] = 1.0
        wf[nl + 1, t, t + 1:] = 1.0
        wb[nl, t, t:] = 1.0
        wb[nl + 1, t, :t] = 1.0
    tile = lambda m: np.tile(m, (1, 1, GLA_HEADS))
    return (wf.reshape(-1, c), wb.reshape(-1, c), tile(mf), tile(mb))


def _gla_gate(ga, w2_ref, b2_ref):
    a_hi, a_lo = _split_bf16(ga)
    x = _bdot(jnp.concatenate([a_hi, a_lo, a_hi], axis=1), w2_ref[...]) + b2_ref[...]
    return (jnp.minimum(x, 0.0) - jnp.log(1.0 + jnp.exp(-jnp.abs(x)))) * (1.0 / GLA_TAU)


def _block_diag(blocks):
    n = len(blocks)
    zero = jnp.zeros_like(blocks[0])
    return jnp.concatenate(
        [jnp.concatenate([blocks[i] if i == j else zero for j in range(n)], axis=1) for i in range(n)],
        axis=0)


_TN = (((0,), (0,)), ((), ()))
_NT = (((1,), (1,)), ((), ()))


class _GlaDir:
    def __init__(self, qs, ks, vs, gs, wall_ref, mask_ref, head_mask, ones_ref, diag_level):
        self.q, self.k, self.v, self.g = qs, ks, vs, gs
        self.wall_ref, self.mask_ref, self.head_mask, self.ones_ref = wall_ref, mask_ref, head_mask, ones_ref
        self.n_levels = len(GLA_LEVELS) + (1 if diag_level else 0)
        self.n = len(qs)

    def exponents(self):
        self.g2 = [jnp.concatenate(_split_bf16(g), axis=0) for g in self.g]
        e = _bdot(self.wall_ref[...], jnp.concatenate(self.g2, axis=1))
        f = jnp.exp(e)
        self.f = [f[:, i * GLA_QK:(i + 1) * GLA_QK] for i in range(self.n)]

    def intra(self):
        c, nl = GLA_CHUNK, len(GLA_LEVELS)
        atts = []
        for q, k, f in zip(self.q, self.k, self.f):
            att = None
            for li in range(self.n_levels):
                if li < nl:
                    fl = f[li * c:(li + 1) * c]
                    qs, ks = (q * fl).astype(BF16), (k * fl).astype(BF16)
                else:
                    qs, ks = q.astype(BF16), k.astype(BF16)
                kblk = jnp.where(self.head_mask, jnp.concatenate([ks] * GLA_HEADS, axis=0),
                                 jnp.zeros((), BF16))
                r = lax.dot_general(qs, kblk, _NT, preferred_element_type=F32) * self.mask_ref[li]
                att = r if att is None else att + r
            atts.append(att.astype(BF16))
        self.o = [_bdot(att, _block_diag([v[:, h * GLA_DV:(h + 1) * GLA_DV] for h in range(GLA_HEADS)]))
                  for att, v in zip(atts, self.v)]

    def state_terms(self):
        c, nl = GLA_CHUNK, len(GLA_LEVELS)
        self.qb = [(q * f[nl * c:(nl + 1) * c]).astype(BF16) for q, f in zip(self.q, self.f)]
        self.upd, self.dec = [], []
        for k, v, f, g2 in zip(self.k, self.v, self.f, self.g2):
            ke = (k * f[(nl + 1) * c:(nl + 2) * c]).astype(BF16)
            self.upd.append(lax.dot_general(ke, v, _TN, preferred_element_type=F32))
            self.dec.append(jnp.exp(lax.dot_general(g2, self.ones_ref[...], _TN,
                                                    preferred_element_type=F32)))

    def scan(self, state_ref):
        states = [state_ref[h] for h in range(GLA_HEADS)]
        outs = []
        for i in range(self.n):
            sblk = _block_diag([s.astype(BF16) for s in states])
            outs.append(self.o[i] + _bdot(self.qb[i], sblk))
            states = [states[h] * self.dec[i][h * GLA_DK:(h + 1) * GLA_DK]
                      + self.upd[i][h * GLA_DK:(h + 1) * GLA_DK, h * GLA_DV:(h + 1) * GLA_DV]
                      for h in range(GLA_HEADS)]
        for h in range(GLA_HEADS):
            state_ref[h] = states[h]
        return outs


def _gla_begin(first, gaf_ref, gab_ref, w2f_ref, b2f_ref, w2b_ref, b2b_ref, gf_scr, gb_scr, st_f, st_b):
    @pl.when(first)
    def _():
        st_f[...] = jnp.zeros(st_f.shape, st_f.dtype)
        st_b[...] = jnp.zeros(st_b.shape, st_b.dtype)

    gf_scr[...] = _gla_gate(gaf_ref[...], w2f_ref, b2f_ref)
    gb_scr[...] = _gla_gate(gab_ref[...], w2b_ref, b2b_ref)


def _gla_head_mask():
    hc = GLA_HEADS * GLA_CHUNK
    return (lax.broadcasted_iota(jnp.int32, (hc, GLA_QK), 0) // GLA_CHUNK
            == lax.broadcasted_iota(jnp.int32, (hc, GLA_QK), 1) // GLA_DK)


def _gla_trip_tasks(first_f, first_b, fwd_refs, bwd_refs, head_mask, ones_ref):
    qscale = GLA_DK ** -0.5

    def rows_of(start, step):
        out = []
        for u in range(GLA_UNROLL):
            r = (start + step * u) * GLA_CHUNK
            out.append(pl.ds(r if isinstance(r, int) else pl.multiple_of(r, GLA_CHUNK), GLA_CHUNK))
        return out

    def direction(refs, rows, diag_level):
        q_ref, k_ref, v_ref, g_scr, wall_ref, mask_ref = refs[:6]
        data = [(q_ref[r, :].astype(F32) * qscale, k_ref[r, :].astype(F32), v_ref[r, :], g_scr[r, :])
                for r in rows]
        return _GlaDir(*zip(*data), wall_ref, mask_ref, head_mask, ones_ref, diag_level)

    rows_f, rows_b = rows_of(first_f, 1), rows_of(first_b, -1)
    dirs = {}

    def begin(name, refs, rows, diag_level):
        dirs[name] = direction(refs, rows, diag_level)
        dirs[name].exponents()

    def finish(name, refs, rows):
        o_ref, state_ref = refs[6:]
        for r, o in zip(rows, dirs[name].scan(state_ref)):
            o_ref[r, :] = o.astype(o_ref.dtype)

    return [
        functools.partial(begin, "f", fwd_refs, rows_f, True),
        functools.partial(begin, "b", bwd_refs, rows_b, False),
        lambda: dirs["f"].intra(), lambda: dirs["b"].intra(),
        lambda: dirs["f"].state_terms(), lambda: dirs["b"].state_terms(),
        functools.partial(finish, "f", fwd_refs, rows_f),
        functools.partial(finish, "b", bwd_refs, rows_b),
    ]


def _gla_kernel(qf_ref, kf_ref, vf_ref, gaf_ref, qb_ref, kb_ref, vb_ref, gab_ref,
                w2f_ref, b2f_ref, w2b_ref, b2b_ref, wallf_ref, wallb_ref, mf_ref, mb_ref, ones_ref,
                of_ref, ob_ref, gf_scr, gb_scr, st_f, st_b, *, n_chunks):
    _gla_begin(pl.program_id(1) == 0, gaf_ref, gab_ref, w2f_ref, b2f_ref, w2b_ref, b2b_ref,
               gf_scr, gb_scr, st_f, st_b)
    head_mask = _gla_head_mask()
    fwd_refs = (qf_ref, kf_ref, vf_ref, gf_scr, wallf_ref, mf_ref, of_ref, st_f)
    bwd_refs = (qb_ref, kb_ref, vb_ref, gb_scr, wallb_ref, mb_ref, ob_ref, st_b)

    def body(t, carry):
        for task in _gla_trip_tasks(t * GLA_UNROLL, n_chunks - 1 - t * GLA_UNROLL, fwd_refs, bwd_refs,
                                    head_mask, ones_ref):
            task()
        return carry

    lax.fori_loop(0, n_chunks // GLA_UNROLL, body, 0)


def _gla(p_main, p_ga, w2f, b2f, w2b, b2b, batch, seq):
    t = p_main.shape[0]
    rb = min(512, seq)
    nblk = seq // rb
    n_chunks = rb // GLA_CHUNK
    wallf, wallb, mf, mb = _gla_constants()
    wallf = jnp.asarray(np.concatenate([wallf, wallf], axis=1), BF16)
    wallb = jnp.asarray(np.concatenate([wallb, wallb], axis=1), BF16)
    mf, mb = jnp.asarray(mf, F32), jnp.asarray(mb, F32)
    ones = jnp.ones((2 * GLA_CHUNK, GLA_DV), BF16)

    fwd = lambda b, n: b * nblk + n
    bwd = lambda b, n: b * nblk + (nblk - 1 - n)

    def row_specs(rowf):
        return [
            pl.BlockSpec((rb, GLA_QK), lambda b, n: (rowf(b, n), E_GQ // GLA_QK)),
            pl.BlockSpec((rb, GLA_QK), lambda b, n: (rowf(b, n), E_GK // GLA_QK)),
            pl.BlockSpec((rb, GLA_WIDTH), lambda b, n: (rowf(b, n), E_GV // GLA_WIDTH)),
            pl.BlockSpec((rb, E_GA), lambda b, n: (rowf(b, n), 0)),
        ]

    full = lambda a: pl.BlockSpec(a.shape, lambda b, n: (0,) * a.ndim)
    consts = (w2f, b2f, w2b, b2b, wallf, wallb, mf, mb, ones)
    return pl.pallas_call(
        functools.partial(_gla_kernel, n_chunks=n_chunks),
        grid=(batch, nblk),
        in_specs=row_specs(fwd) + row_specs(bwd) + [full(a) for a in consts],
        out_specs=[
            pl.BlockSpec((rb, GLA_WIDTH), lambda b, n: (fwd(b, n), 0)),
            pl.BlockSpec((rb, GLA_WIDTH), lambda b, n: (bwd(b, n), 0)),
        ],
        out_shape=[jax.ShapeDtypeStruct((t, GLA_WIDTH), BF16)] * 2,
        scratch_shapes=[
            pltpu.VMEM((rb, GLA_QK), F32), pltpu.VMEM((rb, GLA_QK), F32),
            pltpu.VMEM((GLA_HEADS, GLA_DK, GLA_DV), F32), pltpu.VMEM((GLA_HEADS, GLA_DK, GLA_DV), F32),
        ],
        compiler_params=_cparams(("arbitrary", "arbitrary")),
        name="gla_bidir",
    )(p_main, p_main, p_main, p_ga, p_main, p_main, p_main, p_ga, *consts)


def _rms(x, g):
    return (x * lax.rsqrt(jnp.mean(x * x, axis=-1, keepdims=True) + EPS)) * g


def _mla_prep_kernel(cq_ref, ckv_ref, kr_ref, csn_ref, gq_ref, wqt_ref, gkv_ref, wkn_ref,
                     wvt_ref, qt_ref, k_ref, vt_ref):
    scale = (MLA_NOPE + MLA_ROPE) ** -0.5 * math.log2(math.e)
    hw = MLA_HEADS * MLA_NOPE
    hr = MLA_HEADS * MLA_ROPE
    tm = cq_ref.shape[0]
    cqn =_rms(cq_ref[...].astype(F32), gq_ref[...]).astype(BF16)
    qt = lax.dot_general(wqt_ref[...], cqn, _NT, preferred_element_type=F32)
    csn = csn_ref[...]
    csn_t = csn.T
    cs2 = jnp.concatenate([csn_t[:MLA_ROPE]] * MLA_HEADS, axis=0)
    sn2 = jnp.concatenate([csn_t[MLA_ROPE:]] * MLA_HEADS, axis=0)
    qr = (qt[hw:hw + hr] * cs2 + qt[hw + hr:hw + 2 * hr] * sn2) * scale
    kvn = _rms(ckv_ref[...].astype(F32), gkv_ref[...]).astype(BF16)
    kn = _bdot(kvn, wkn_ref[...])
    vt = lax.dot_general(wvt_ref[...], kvn, _NT, preferred_element_type=F32)
    kt = kr_ref[...].astype(F32) * csn
    kr = (kt[:, :MLA_ROPE] + kt[:, MLA_ROPE:]).astype(BF16)
    npad = MLA_QK_PAD - MLA_NOPE - MLA_ROPE
    for h in range(MLA_HEADS):
        qt_ref[0, h, 0:MLA_NOPE, :] = (qt[h * MLA_NOPE:(h + 1) * MLA_NOPE] * scale).astype(BF16)
        qt_ref[0, h, MLA_NOPE:MLA_NOPE + MLA_ROPE, :] = qr[h * MLA_ROPE:(h + 1) * MLA_ROPE].astype(BF16)
        qt_ref[0, h, MLA_NOPE + MLA_ROPE:, :] = jnp.zeros((npad, tm), BF16)
        k_ref[0, h, :, 0:MLA_NOPE] = kn[:, h * MLA_NOPE:(h + 1) * MLA_NOPE].astype(BF16)
        k_ref[0, h, :, MLA_NOPE:MLA_NOPE + MLA_ROPE] = kr
        k_ref[0, h, :, MLA_NOPE + MLA_ROPE:] = jnp.zeros((tm, npad), BF16)
        vt_ref[0, h] = vt[h * MLA_V:(h + 1) * MLA_V].astype(BF16)


def _mla_prep(p_main, csn, g_q, w_uqt, g_kv, w_kn, w_vt, batch, seq):
    tm = min(512, seq)
    per_b = seq // tm
    row = lambda b, i: b * per_b + i
    full = lambda a: pl.BlockSpec(a.shape, lambda b, i: (0,) * a.ndim)
    return pl.pallas_call(
        _mla_prep_kernel,
        grid=(batch, per_b),
        in_specs=[
            pl.BlockSpec((tm, MLA_Q_RANK), lambda b, i: (row(b, i), E_CQ // MLA_Q_RANK)),
            pl.BlockSpec((tm, MLA_KV_RANK), lambda b, i: (row(b, i), E_CKV // MLA_KV_RANK)),
            pl.BlockSpec((tm, LANE), lambda b, i: (row(b, i), E_KR // LANE)),
            pl.BlockSpec((tm, LANE), lambda b, i: (row(b, i), 0)),
            full(g_q), full(w_uqt), full(g_kv), full(w_kn), full(w_vt),
        ],
        out_specs=[
            pl.BlockSpec((1, MLA_HEADS, MLA_QK_PAD, tm), lambda b, i: (b, 0, 0, i)),
            pl.BlockSpec((1, MLA_HEADS, tm, MLA_QK_PAD), lambda b, i: (b, 0, i, 0)),
            pl.BlockSpec((1, MLA_HEADS, MLA_V, tm), lambda b, i: (b, 0, 0, i)),
        ],
        out_shape=[
            jax.ShapeDtypeStruct((batch, MLA_HEADS, MLA_QK_PAD, seq), BF16),
            jax.ShapeDtypeStruct((batch, MLA_HEADS, seq, MLA_QK_PAD), BF16),
            jax.ShapeDtypeStruct((batch, MLA_HEADS, MLA_V, seq), BF16),
        ],
        compiler_params=_cparams(("arbitrary", "arbitrary")),
        name="mla_prep",
    )(p_main, p_main, p_main, csn, g_q, w_uqt, g_kv, w_kn, w_vt)


class _AttnGroup:
    def __init__(self, c, qt_ref, k_ref, vt_ref, z_ref, o_ref):
        self.seq = k_ref.shape[2]
        self.tk = min(MLA_TK, self.seq)
        self.sub = min(MLA_TQ, qt_ref.shape[3])
        self.cols = slice(c * self.sub, (c + 1) * self.sub)
        self.qt_ref, self.k_ref, self.vt_ref, self.z_ref, self.o_ref = qt_ref, k_ref, vt_ref, z_ref, o_ref

        self.nk = self.seq // self.tk
        self.s, self.m, self.acc = [], None, None

    def score_block(self, j):
        qt = self.qt_ref[0, 0, :, self.cols]
        s = _bdot(self.k_ref[0, 0, j * self.tk:(j + 1) * self.tk, :], qt)
        mj = jnp.max(s, axis=0, keepdims=True)
        self.s.append(s)
        self.m = mj if self.m is None else jnp.maximum(self.m, mj)

    def value_block(self, j):
        ones = jnp.ones((MLA_ONES_ROWS, self.tk), BF16)
        p = jnp.exp2(self.s[j] - self.m).astype(BF16)
        va = jnp.concatenate([self.vt_ref[0, 0, :, j * self.tk:(j + 1) * self.tk], ones], axis=0)
        oj = _bdot(va, p)
        self.acc = oj if self.acc is None else self.acc + oj
        if j == self.nk - 1:
            ot = self.acc[:MLA_V] / self.acc[MLA_V:MLA_V + 1]
            gate = _silu(self.z_ref[self.cols, :].astype(F32))
            self.o_ref[self.cols, :] = (ot.T * gate).astype(self.o_ref.dtype)

    def score_tasks(self):
        return [functools.partial(self.score_block, j) for j in range(self.nk)]

    def value_tasks(self):
        return [functools.partial(self.value_block, j) for j in range(self.nk)]


def _attn_groups(qt_ref, k_ref, vt_ref, z_ref, o_ref):
    n = qt_ref.shape[3] // min(MLA_TQ, qt_ref.shape[3])
    return [_AttnGroup(c, qt_ref, k_ref, vt_ref, z_ref, o_ref) for c in range(n)]


def _attn_tasks(groups):
    tasks = list(groups[0].score_tasks())
    for c in range(1, len(groups)):
        tasks += groups[c].score_tasks() + groups[c - 1].value_tasks()
    return tasks + groups[-1].value_tasks()


def _mla_attn_kernel(qt_ref, k_ref, vt_ref, z_ref, o_ref):
    for task in _attn_tasks(_attn_groups(qt_ref, k_ref, vt_ref, z_ref, o_ref)):
        task()


def _mla_attention(q_t, k_cat, v_t, p_main, batch, seq):
    tq = min(MLA_TQ * MLA_GROUPS, seq)
    nq = seq // tq
    return pl.pallas_call(
        _mla_attn_kernel,
        grid=(batch, MLA_HEADS, nq),
        in_specs=[
            pl.BlockSpec((1, 1, MLA_QK_PAD, tq), lambda b, h, i: (b, h, 0, i)),
            pl.BlockSpec((1, 1, seq, MLA_QK_PAD), lambda b, h, i: (b, h, 0, 0)),
            pl.BlockSpec((1, 1, MLA_V, seq), lambda b, h, i: (b, h, 0, 0)),
            pl.BlockSpec((tq, MLA_V), lambda b, h, i: (b * nq + i, E_MZ // MLA_V + h)),
        ],
        out_specs=pl.BlockSpec((tq, MLA_V), lambda b, h, i: (b * nq + i, h)),
        out_shape=jax.ShapeDtypeStruct((batch * seq, MLA_WIDTH), BF16),
        compiler_params=_cparams(("arbitrary", "arbitrary", "arbitrary")),
        name="mla_attention",
    )(q_t, k_cat, v_t, p_main)


def _even_out_kernel(x_ref, of_ref, ob_ref, gz_ref, om_ref, mod_ref, gn_ref, w_ref, gp_ref, o_ref, *, d):
    oa = of_ref[...].astype(F32) + ob_ref[...].astype(F32)
    gn = gn_ref[...]
    parts = [_rms(oa[:, h * GLA_DV:(h + 1) * GLA_DV], gn) for h in range(GLA_HEADS)]
    oa = jnp.concatenate(parts, axis=-1) * _silu(gz_ref[...].astype(F32))
    y = _bdot(oa.astype(BF16), w_ref[0:GLA_WIDTH, :]) + _bdot(om_ref[...], w_ref[GLA_WIDTH:, :])
    o_ref[...] = _postnorm_residual(x_ref[...], y, mod_ref, gp_ref, d)


def _even_out_proj(x2, o_f, o_b, p_main, o_mla, mod_l, g_norm, w_out, g_post, seq):
    t, d = x2.shape
    tm = min(512, seq)
    per_b = seq // tm
    return pl.pallas_call(
        functools.partial(_even_out_kernel, d=d),
        grid=(t // tm,),
        in_specs=[
            pl.BlockSpec((tm, d), lambda i: (i, 0)),
            pl.BlockSpec((tm, GLA_WIDTH), lambda i: (i, 0)),
            pl.BlockSpec((tm, GLA_WIDTH), lambda i: (i, 0)),
            pl.BlockSpec((tm, GLA_WIDTH), lambda i: (i, E_GZ // GLA_WIDTH)),
            pl.BlockSpec((tm, MLA_WIDTH), lambda i: (i, 0)),
            pl.BlockSpec((1, 1, 3 * d), lambda i: (i // per_b, 0, 0)),
            pl.BlockSpec((1, GLA_DV), lambda i: (0, 0)),
            pl.BlockSpec((GLA_WIDTH + MLA_WIDTH, d), lambda i: (0, 0)),
            pl.BlockSpec((1, d), lambda i: (0, 0)),
        ],
        out_specs=pl.BlockSpec((tm, d), lambda i: (i, 0)),
        out_shape=jax.ShapeDtypeStruct((t, d), F32),
        compiler_params=_cparams(("arbitrary",)),
        name="even_out_proj",
    )(x2, o_f, o_b, p_main, o_mla, mod_l, g_norm, w_out, g_post)


def _odd_kernel(x_ref, mod_ref, gpre_ref, win_ref, gln_ref, bln_ref, ws_ref, bs_ref, wout_ref, gpost_ref,
                o_ref, *, d, width):
    x = x_ref[...]
    tm = x.shape[0]
    hb = _prenorm(x, mod_ref, gpre_ref, d).astype(BF16)
    gw = width // SG_GROUPS

    ncb = width // ODD_COLS
    proj = lambda base, c: _bdot(hb, win_ref[:, base + c * ODD_COLS:base + (c + 1) * ODD_COLS])
    vp = [proj(width, c) for c in range(ncb)]
    up, v = [], []
    for c in range(ncb):
        up.append(proj(0, c))
        v.append(_gelu(vp[c]))
    v = jnp.concatenate(v, axis=-1)
    mu = jnp.mean(v, axis=-1, keepdims=True)
    vc = v - mu
    var = jnp.mean(vc * vc, axis=-1, keepdims=True)
    vn = ((vc * lax.rsqrt(var + EPS)) * gln_ref[...] + bln_ref[...]).astype(BF16)
    zp, u = [], []
    for c in range(ncb):
        zp.append(proj(2 * width, c))
        u.append(_gelu(up[c]))

    rows = []
    for c in range(tm // SG_CHUNK):
        cols = []
        for g in range(SG_GROUPS):
            blk = vn[c * SG_CHUNK:(c + 1) * SG_CHUNK, g * gw:(g + 1) * gw]
            cols.append(_bdot(ws_ref[g], blk) + bs_ref[:, g:g + 1])
        rows.append(jnp.concatenate(cols, axis=-1))
    mixed = jnp.concatenate(rows, axis=0) if len(rows) > 1 else rows[0]

    gated = [(u[c] * mixed[:, c * ODD_COLS:(c + 1) * ODD_COLS] * _silu(zp[c])).astype(BF16)
             for c in range(ncb)]
    y = _bdot(jnp.concatenate(gated, axis=-1), wout_ref[...])
    o_ref[...] = _postnorm_residual(x, y, mod_ref, gpost_ref, d)


def _odd_layer(x2, mod_l, g_pre, w_in, g_ln, b_ln, w_s, b_s_t, w_out, g_post, seq, layer):
    t, d = x2.shape
    width = w_out.shape[1]
    tm = min(ODD_ROWS, seq)
    per_b = seq // tm
    full = lambda a: pl.BlockSpec(a.shape, lambda i: (0,) * a.ndim, pipeline_mode=pl.Buffered(1))
    stacked = lambda a: pl.BlockSpec((None,) + a.shape[1:], lambda i: (layer,) + (0,) * (a.ndim - 1),
                                     pipeline_mode=pl.Buffered(1))
    return pl.pallas_call(
        functools.partial(_odd_kernel, d=d, width=width),
        grid=(t // tm,),
        in_specs=[
            pl.BlockSpec((tm, d), lambda i: (i, 0)),
            pl.BlockSpec((1, 1, 3 * d), lambda i: (i // per_b, 0, 0)),
            full(g_pre), stacked(w_in), full(g_ln), full(b_ln), stacked(w_s), full(b_s_t), stacked(w_out),
            full(g_post),
        ],
        out_specs=pl.BlockSpec((tm, d), lambda i: (i, 0)),
        out_shape=jax.ShapeDtypeStruct((t, d), F32),
        compiler_params=_cparams(("arbitrary",)),
        name="odd_sgmlp",
    )(x2, mod_l, g_pre, w_in, g_ln, b_ln, w_s, b_s_t, w_out, g_post)


def _swap_halves(w):
    half = w.shape[-1] // 2
    return jnp.concatenate([w[..., half:], w[..., :half]], axis=-1)


def _even_in_weight(w):
    gq, gk, gv = w[:, 0:256], w[:, 256:512], w[:, 512:1024]
    ga_f, ga_b = w[:, 1024:1040], w[:, 1040:1056]
    gz, cq, ckv, kr, mz = w[:, 1056:1568], w[:, 1568:1824], w[:, 1824:1952], w[:, 1952:2016], w[:, 2016:2528]
    pad = jnp.zeros((w.shape[0], E_GA - 2 * GLA_GATE_RANK), w.dtype)
    return jnp.concatenate([gq, gk, gv, gz, mz, cq, ckv, kr, _swap_halves(kr), ga_f, ga_b, pad],
                           axis=-1).astype(BF16)


def _uq_weight(w):
    w3 = w.reshape(w.shape[0], MLA_HEADS, MLA_NOPE + MLA_ROPE)
    nope = w3[:, :, :MLA_NOPE].reshape(w.shape[0], -1)
    rope = w3[:, :, MLA_NOPE:]
    return jnp.concatenate([nope, rope.reshape(w.shape[0], -1), _swap_halves(rope).reshape(w.shape[0], -1)],
                           axis=-1).astype(BF16).T


def _ukv_weights(w):
    w3 = w.reshape(w.shape[0], MLA_HEADS, MLA_NOPE + MLA_V)
    w_kn = w3[:, :, :MLA_NOPE].reshape(w.shape[0], -1)
    w_v = w3[:, :, MLA_NOPE:].reshape(w.shape[0], -1)
    return w_kn.astype(BF16), w_v.astype(BF16).T


def _gate_weight(w, offset):
    out = jnp.zeros((E_GA, w.shape[1]), F32).at[offset:offset + w.shape[0]].set(w)
    hi, lo = _split_bf16(out)
    return jnp.concatenate([hi, hi, lo], axis=0)


def _rope_table(positions):
    half = MLA_ROPE // 2
    inv_freq = 1.0 / (ROPE_THETA ** (jnp.arange(0, MLA_ROPE, 2, dtype=F32) / MLA_ROPE))
    phase = jnp.concatenate([jnp.zeros((MLA_ROPE,), F32), jnp.full((half,), math.pi / 2, F32),
                             jnp.full((half,), -math.pi / 2, F32)])
    ang = positions.astype(F32).reshape(-1, 1) * jnp.tile(inv_freq, 4)
    return jnp.cos(ang + phase)


def kernel(x, c, positions, w_mod, b_mod, g_pre, g_post, w_in_e, gla_w_gate_f, gla_b_gate_f, gla_w_gate_b, gla_b_gate_b, gla_g_norm, mla_g_q, mla_w_uq, mla_g_kv, mla_w_ukv, w_out_e, w_in_o, sg_g_norm, sg_b_norm, sg_w_s, sg_b_s, w_out_o):
    batch, seq, d = x.shape
    depth = w_mod.shape[0]
    mod = _modulation(c, w_mod, b_mod)
    csn = _rope_table(positions)
    w_in_o_b, w_s_b, w_out_o_b = w_in_o.astype(BF16), sg_w_s.astype(BF16), w_out_o.astype(BF16)
    x2 = x.reshape(batch * seq, d)
    row = lambda a: a.reshape(1, -1)
    for l in range(depth):
        mod_l = mod[l][:, None, :]
        if l % 2 == 0:
            e = l // 2
            p_main, p_ga = _even_in_proj(x2, mod_l, row(g_pre[l]), _even_in_weight(w_in_e[e]), seq)
            w_kn, w_vt = _ukv_weights(mla_w_ukv[e])
            q_t, k_cat, v_t = _mla_prep(p_main, csn, row(mla_g_q[e]), _uq_weight(mla_w_uq[e]),
                                        row(mla_g_kv[e]), w_kn, w_vt, batch, seq)
            o_f, o_b = _gla(p_main, p_ga,
                            _gate_weight(gla_w_gate_f[e], 0), row(gla_b_gate_f[e]),
                            _gate_weight(gla_w_gate_b[e], GLA_GATE_RANK), row(gla_b_gate_b[e]),
                            batch, seq)
            o_mla = _mla_attention(q_t, k_cat, v_t, p_main, batch, seq)
            x2 = _even_out_proj(x2, o_f, o_b, p_main, o_mla, mod_l, row(gla_g_norm[e]),
                                w_out_e[e].astype(BF16), row(g_post[l]), seq)
        else:
            o = l // 2
            x2 = _odd_layer(x2, mod_l, row(g_pre[l]), w_in_o_b, row(sg_g_norm[o]), row(sg_b_norm[o]),
                            w_s_b, sg_b_s[o].T, w_out_o_b, row(g_post[l]), seq, o)
    return x2.reshape(batch, seq, d)
```

```python
import functools
import math

import numpy as np
import jax
import jax.numpy as jnp
from jax import lax
from jax.experimental import pallas as pl
from jax.experimental.pallas import tpu as pltpu

F32 = jnp.float32
BF16 = jnp.bfloat16

EPS = 1e-6
GLA_HEADS = 4
GLA_DK = 64
GLA_DV = 128
GLA_GATE_RANK = 16
GLA_TAU = 16.0
GLA_CHUNK = 64
GLA_QK = GLA_HEADS * GLA_DK
GLA_WIDTH = GLA_HEADS * GLA_DV
MLA_HEADS = 4
MLA_Q_RANK = 256
MLA_KV_RANK = 128
MLA_NOPE = 128
MLA_ROPE = 64
MLA_V = 128
MLA_WIDTH = MLA_HEADS * MLA_V
MLA_QK_PAD = 256
MLA_TQ = 512
MLA_GROUPS = 8
MLA_TK = 512
MLA_ONES_ROWS = 16
ROPE_THETA = 10000.0
SG_GROUPS = 8
SG_CHUNK = 128
ODD_COLS = 512
ODD_ROWS = 512

LANE = 128
VMEM_LIMIT = 56 * 1024 * 1024

E_GQ, E_GK, E_GV, E_GZ, E_MZ, E_CQ, E_CKV, E_KR = 0, 256, 512, 1024, 1536, 2048, 2304, 2432
E_MAIN = 2560
E_GA = 128
E_TOTAL = E_MAIN + E_GA

GLA_LEVELS = (32, 16, 8, 4, 2, 1)
GLA_UNROLL = 2


def _cparams(sem, flags=None):
    return pltpu.CompilerParams(dimension_semantics=sem, vmem_limit_bytes=VMEM_LIMIT, flags=flags)


def _silu(x):
    return x * jax.nn.sigmoid(x)


def _gelu(x):
    return 0.5 * x * (1.0 + lax.erf(x * (2.0 ** -0.5)))


def _bdot(a, b):
    return jnp.dot(a, b, preferred_element_type=F32)


def _split_bf16(x):
    hi = x.astype(BF16)
    lo = (x - hi.astype(F32)).astype(BF16)
    return hi, lo


def _mod_kernel(c_ref, w_ref, b_ref, o_ref):
    ca = _silu(c_ref[...])
    o_ref[0] = jnp.dot(ca, w_ref[0], precision=lax.Precision.HIGHEST,
                       preferred_element_type=F32) + b_ref[0]


def _modulation(c, w_mod, b_mod):
    depth, d, n = w_mod.shape
    b = c.shape[0]
    tn = 1024
    return pl.pallas_call(
        _mod_kernel,
        grid=(depth, n // tn),
        in_specs=[
            pl.BlockSpec((b, d), lambda l, j: (0, 0)),
            pl.BlockSpec((1, d, tn), lambda l, j: (l, 0, j)),
            pl.BlockSpec((1, 1, tn), lambda l, j: (l, 0, j)),
        ],
        out_specs=pl.BlockSpec((1, b, tn), lambda l, j: (l, 0, j)),
        out_shape=jax.ShapeDtypeStruct((depth, b, n), F32),
        compiler_params=_cparams(("arbitrary", "arbitrary")),
        name="adaln_mod",
    )(c, w_mod, b_mod.reshape(depth, 1, n))


def _prenorm(x, mod_ref, g_ref, d):
    ms = jnp.mean(x * x, axis=-1, keepdims=True)
    shift = mod_ref[0, :, 0:d]
    scale = mod_ref[0, :, d:2 * d]
    return (x * lax.rsqrt(ms + EPS)) * g_ref[...] * (1.0 + scale) + shift


def _postnorm_residual(x, y, mod_ref, g_ref, d):
    ms = jnp.mean(y * y, axis=-1, keepdims=True)
    gate = mod_ref[0, :, 2 * d:3 * d]
    return x + gate * ((y * lax.rsqrt(ms + EPS)) * g_ref[...])


def _even_in_kernel(x_ref, mod_ref, g_ref, w_ref, o_ref, ga_ref, *, d):
    hb = _prenorm(x_ref[...], mod_ref, g_ref, d).astype(BF16)
    for c0 in range(0, E_MAIN, 512):
        o_ref[:, c0:c0 + 512] = _bdot(hb, w_ref[:, c0:c0 + 512]).astype(BF16)
    ga_ref[...] = _bdot(hb, w_ref[:, E_MAIN:E_TOTAL])


def _even_in_proj(x2, mod_l, g_pre, w_e, seq):
    t, d = x2.shape
    tm = min(512, seq)
    per_b = seq // tm
    return pl.pallas_call(
        functools.partial(_even_in_kernel, d=d),
        grid=(t // tm,),
        in_specs=[
            pl.BlockSpec((tm, d), lambda i: (i, 0)),
            pl.BlockSpec((1, 1, 3 * d), lambda i: (i // per_b, 0, 0)),
            pl.BlockSpec((1, d), lambda i: (0, 0)),
            pl.BlockSpec((d, E_TOTAL), lambda i: (0, 0)),
        ],
        out_specs=[
            pl.BlockSpec((tm, E_MAIN), lambda i: (i, 0)),
            pl.BlockSpec((tm, E_GA), lambda i: (i, 0)),
        ],
        out_shape=[
            jax.ShapeDtypeStruct((t, E_MAIN), BF16),
            jax.ShapeDtypeStruct((t, E_GA), F32),
        ],
        compiler_params=_cparams(("arbitrary",)),
        name="even_in_proj",
    )(x2, mod_l, g_pre, w_e)


def _gla_constants():
    c = GLA_CHUNK
    idx = np.arange(c)
    nl = len(GLA_LEVELS)
    wf = np.zeros((nl + 2, c, c), np.float32)
    wb = np.zeros((nl + 2, c, c), np.float32)
    mf = np.zeros((nl + 1, c, c), np.float32)
    mb = np.zeros((nl, c, c), np.float32)
    for li, s in enumerate(GLA_LEVELS):
        mid = (idx // (2 * s)) * (2 * s) + s
        upper = (idx % (2 * s)) >= s
        for t in range(c):
            if upper[t]:
                wf[li, t, mid[t]:t + 1] = 1.0
                wb[li, t, mid[t]:t] = 1.0
            else:
                wf[li, t, t + 1:mid[t]] = 1.0
                wb[li, t, t:mid[t]] = 1.0
        same = (idx[:, None] // (2 * s)) == (idx[None, :] // (2 * s))
        mf[li] = same & upper[:, None] & (~upper[None, :])
        mb[li] = same & (~upper[:, None]) & upper[None, :]
    mf[nl] = np.eye(c)
    for t in range(c):
        wf[nl, t, :t + 1] = 1.0
        wf[nl + 1, t, t + 1:] = 1.0
        wb[nl, t, t:] = 1.0
        wb[nl + 1, t, :t] = 1.0
    tile = lambda m: np.tile(m, (1, 1, GLA_HEADS))
    return (wf.reshape(-1, c), wb.reshape(-1, c), tile(mf), tile(mb))


def _gla_gate(ga, w2_ref, b2_ref):
    a_hi, a_lo = _split_bf16(ga)
    x = _bdot(jnp.concatenate([a_hi, a_lo, a_hi], axis=1), w2_ref[...]) + b2_ref[...]
    return (jnp.minimum(x, 0.0) - jnp.log(1.0 + jnp.exp(-jnp.abs(x)))) * (1.0 / GLA_TAU)


def _block_diag(blocks):
    n = len(blocks)
    zero = jnp.zeros_like(blocks[0])
    return jnp.concatenate(
        [jnp.concatenate([blocks[i] if i == j else zero for j in range(n)], axis=1) for i in range(n)],
        axis=0)


_TN = (((0,), (0,)), ((), ()))
_NT = (((1,), (1,)), ((), ()))


def _pair_dot(lhs, blocks):
    outs = []
    for p in range(len(blocks) // 2):
        outs.append(_bdot(lhs[:, p * LANE:(p + 1) * LANE], _block_diag(blocks[2 * p:2 * p + 2])))
    return jnp.concatenate(outs, axis=1)


class _GlaDir:
    def __init__(self, qs, ks, vs, gs, wall_ref, mask_ref, head_mask, forward):
        self.q, self.k, self.v, self.g = qs, ks, vs, gs
        self.wall_ref, self.mask_ref, self.head_mask, self.forward = wall_ref, mask_ref, head_mask, forward
        self.n_levels = len(GLA_LEVELS) + (1 if forward else 0)
        self.n = len(qs)

    def exponents(self):
        g2 = [jnp.concatenate(_split_bf16(g), axis=0) for g in self.g]
        f = jnp.exp(_bdot(self.wall_ref[...], jnp.concatenate(g2, axis=1)))
        self.f = [f[:, i * GLA_QK:(i + 1) * GLA_QK] for i in range(self.n)]

    def level(self, li):
        c, nl = GLA_CHUNK, len(GLA_LEVELS)
        if li >= self.n_levels:
            return
        if li == 0:
            self.att = [None] * self.n
        for i, (q, k, f) in enumerate(zip(self.q, self.k, self.f)):
            if li < nl:
                fl = f[li * c:(li + 1) * c]
                qs, ks = (q * fl).astype(BF16), (k * fl).astype(BF16)
            else:
                qs, ks = q.astype(BF16), k.astype(BF16)
            kblk = jnp.where(self.head_mask, jnp.concatenate([ks] * GLA_HEADS, axis=0), jnp.zeros((), BF16))
            r = lax.dot_general(qs, kblk, _NT, preferred_element_type=F32) * self.mask_ref[li]
            self.att[i] = r if self.att[i] is None else self.att[i] + r

    def intra_values(self):
        self.o = [_pair_dot(att.astype(BF16), [v[:, h * GLA_DV:(h + 1) * GLA_DV] for h in range(GLA_HEADS)])
                  for att, v in zip(self.att, self.v)]

    def state_terms(self):
        c, nl = GLA_CHUNK, len(GLA_LEVELS)
        self.qb = [(q * f[nl * c:(nl + 1) * c]).astype(BF16) for q, f in zip(self.q, self.f)]
        self.upd, self.dec = [], []
        tot_row = nl * c + (c - 1 if self.forward else 0)
        for k, v, f in zip(self.k, self.v, self.f):
            ke = (k * f[(nl + 1) * c:(nl + 2) * c]).astype(BF16)
            upd = []
            for p in range(GLA_HEADS // 2):
                u = lax.dot_general(ke[:, p * LANE:(p + 1) * LANE], v[:, 2 * p * GLA_DV:(2 * p + 2) * GLA_DV],
                                    _TN, preferred_element_type=F32)
                upd += [u[:GLA_DK, :GLA_DV], u[GLA_DK:, GLA_DV:]]
            self.upd.append(upd)
            self.dec.append(jnp.broadcast_to(f[tot_row:tot_row + 1], (GLA_DV, GLA_QK)).T)

    def scan(self, state_ref):
        states = [state_ref[h] for h in range(GLA_HEADS)]
        outs = []
        for i in range(self.n):
            outs.append(self.o[i] + _pair_dot(self.qb[i], [s.astype(BF16) for s in states]))
            states = [states[h] * self.dec[i][h * GLA_DK:(h + 1) * GLA_DK] + self.upd[i][h]
                      for h in range(GLA_HEADS)]
        for h in range(GLA_HEADS):
            state_ref[h] = states[h]
        return outs


def _gla_head_mask():
    hc = GLA_HEADS * GLA_CHUNK
    return (lax.broadcasted_iota(jnp.int32, (hc, GLA_QK), 0) // GLA_CHUNK
            == lax.broadcasted_iota(jnp.int32, (hc, GLA_QK), 1) // GLA_DK)


def _gla_trip_stages(first_f, first_b, fwd_refs, bwd_refs, head_mask):
    qscale = GLA_DK ** -0.5
    c, u = GLA_CHUNK, GLA_UNROLL
    plan = {
        "f": ([slice((first_f + i) * c, (first_f + i + 1) * c) for i in range(u)],
              slice(first_f * c, (first_f + u) * c), [i * c for i in range(u)], fwd_refs, True),
        "b": ([slice((first_b - i) * c, (first_b - i + 1) * c) for i in range(u)],
              slice((first_b - u + 1) * c, (first_b + 1) * c), [(u - 1 - i) * c for i in range(u)],
              bwd_refs, False),
    }
    gates, dirs = {}, {}

    def gate(name):
        _, span, _, refs, _ = plan[name]
        gates[name] = _gla_gate(refs[3][span, :], refs[4], refs[5])

    def factors(name):
        rows, _, offs, refs, forward = plan[name]
        q_ref, k_ref, v_ref = refs[:3]
        data = [(q_ref[r, :].astype(F32) * qscale, k_ref[r, :].astype(F32), v_ref[r, :],
                 gates[name][o:o + c]) for r, o in zip(rows, offs)]
        dirs[name] = _GlaDir(*zip(*data), refs[6], refs[7], head_mask, forward)
        dirs[name].exponents()

    def finish(name):
        rows, _, _, refs, _ = plan[name]
        o_ref, state_ref = refs[8:]
        for r, o in zip(rows, dirs[name].scan(state_ref)):
            o_ref[r, :] = o.astype(o_ref.dtype)

    both = lambda fn: [functools.partial(fn, "f"), functools.partial(fn, "b")]
    levels = [functools.partial(lambda n, li: dirs[n].level(li), n, li)
              for li in range(len(GLA_LEVELS) + 1) for n in ("f", "b")]
    rest = levels + [lambda: dirs["f"].intra_values(), lambda: dirs["b"].intra_values(),
                     lambda: dirs["f"].state_terms(), lambda: dirs["b"].state_terms()] + both(finish)
    return both(gate), both(factors), rest


def _gla_kernel(qf_ref, kf_ref, vf_ref, gaf_ref, qb_ref, kb_ref, vb_ref, gab_ref,
                w2f_ref, b2f_ref, w2b_ref, b2b_ref, wallf_ref, wallb_ref, mf_ref, mb_ref,
                of_ref, ob_ref, st_f, st_b, *, n_chunks):
    @pl.when(pl.program_id(1) == 0)
    def _():
        st_f[...] = jnp.zeros(st_f.shape, st_f.dtype)
        st_b[...] = jnp.zeros(st_b.shape, st_b.dtype)

    head_mask = _gla_head_mask()
    fwd_refs = (qf_ref, kf_ref, vf_ref, gaf_ref, w2f_ref, b2f_ref, wallf_ref, mf_ref, of_ref, st_f)
    bwd_refs = (qb_ref, kb_ref, vb_ref, gab_ref, w2b_ref, b2b_ref, wallb_ref, mb_ref, ob_ref, st_b)

    trips = [_gla_trip_stages(t * GLA_UNROLL, n_chunks - 1 - t * GLA_UNROLL, fwd_refs, bwd_refs, head_mask)
             for t in range(n_chunks // GLA_UNROLL)]
    n_stages = 3
    for step in range(len(trips) + n_stages - 1):
        for stage in range(n_stages):
            t = step - stage
            if 0 <= t < len(trips):
                for task in trips[t][stage]:
                    task()


def _gla(p_main, p_ga, w2f, b2f, w2b, b2b, batch, seq):
    t = p_main.shape[0]
    rb = min(512, seq)
    nblk = seq // rb
    n_chunks = rb // GLA_CHUNK
    wallf, wallb, mf, mb = _gla_constants()
    wallf = jnp.asarray(np.concatenate([wallf, wallf], axis=1), BF16)
    wallb = jnp.asarray(np.concatenate([wallb, wallb], axis=1), BF16)
    mf, mb = jnp.asarray(mf, F32), jnp.asarray(mb, F32)

    fwd = lambda b, n: b * nblk + n
    bwd = lambda b, n: b * nblk + (nblk - 1 - n)

    def row_specs(rowf):
        return [
            pl.BlockSpec((rb, GLA_QK), lambda b, n: (rowf(b, n), E_GQ // GLA_QK)),
            pl.BlockSpec((rb, GLA_QK), lambda b, n: (rowf(b, n), E_GK // GLA_QK)),
            pl.BlockSpec((rb, GLA_WIDTH), lambda b, n: (rowf(b, n), E_GV // GLA_WIDTH)),
            pl.BlockSpec((rb, E_GA), lambda b, n: (rowf(b, n), 0)),
        ]

    full = lambda a: pl.BlockSpec(a.shape, lambda b, n: (0,) * a.ndim)
    consts = (w2f, b2f, w2b, b2b, wallf, wallb, mf, mb)
    return pl.pallas_call(
        functools.partial(_gla_kernel, n_chunks=n_chunks),
        grid=(batch, nblk),
        in_specs=row_specs(fwd) + row_specs(bwd) + [full(a) for a in consts],
        out_specs=[
            pl.BlockSpec((rb, GLA_WIDTH), lambda b, n: (fwd(b, n), 0)),
            pl.BlockSpec((rb, GLA_WIDTH), lambda b, n: (bwd(b, n), 0)),
        ],
        out_shape=[jax.ShapeDtypeStruct((t, GLA_WIDTH), BF16)] * 2,
        scratch_shapes=[
            pltpu.VMEM((GLA_HEADS, GLA_DK, GLA_DV), F32), pltpu.VMEM((GLA_HEADS, GLA_DK, GLA_DV), F32),
        ],
        compiler_params=_cparams(("arbitrary", "arbitrary")),
        name="gla_bidir",
    )(p_main, p_main, p_main, p_ga, p_main, p_main, p_main, p_ga, *consts)


def _rms(x, g):
    return (x * lax.rsqrt(jnp.mean(x * x, axis=-1, keepdims=True) + EPS)) * g


def _mla_prep_kernel(cq_ref, ckv_ref, kr_ref, csn_ref, gq_ref, wqt_ref, gkv_ref, wkn_ref,
                     wvt_ref, qt_ref, k_ref, vt_ref):
    scale = (MLA_NOPE + MLA_ROPE) ** -0.5 * math.log2(math.e)
    hw = MLA_HEADS * MLA_NOPE
    hr = MLA_HEADS * MLA_ROPE
    tm = cq_ref.shape[0]
    cqn =_rms(cq_ref[...].astype(F32), gq_ref[...]).astype(BF16)
    qt = lax.dot_general(wqt_ref[...], cqn, _NT, preferred_element_type=F32)
    csn = csn_ref[...]
    csn_t = csn.T
    cs2 = jnp.concatenate([csn_t[:MLA_ROPE]] * MLA_HEADS, axis=0)
    sn2 = jnp.concatenate([csn_t[MLA_ROPE:]] * MLA_HEADS, axis=0)
    qr = (qt[hw:hw + hr] * cs2 + qt[hw + hr:hw + 2 * hr] * sn2) * scale
    kvn = _rms(ckv_ref[...].astype(F32), gkv_ref[...]).astype(BF16)
    kn = _bdot(kvn, wkn_ref[...])
    vt = lax.dot_general(wvt_ref[...], kvn, _NT, preferred_element_type=F32)
    kt = kr_ref[...].astype(F32) * csn
    kr = (kt[:, :MLA_ROPE] + kt[:, MLA_ROPE:]).astype(BF16)
    npad = MLA_QK_PAD - MLA_NOPE - MLA_ROPE
    for h in range(MLA_HEADS):
        qt_ref[0, h, 0:MLA_NOPE, :] = (qt[h * MLA_NOPE:(h + 1) * MLA_NOPE] * scale).astype(BF16)
        qt_ref[0, h, MLA_NOPE:MLA_NOPE + MLA_ROPE, :] = qr[h * MLA_ROPE:(h + 1) * MLA_ROPE].astype(BF16)
        qt_ref[0, h, MLA_NOPE + MLA_ROPE:, :] = jnp.zeros((npad, tm), BF16)
        k_ref[0, h, :, 0:MLA_NOPE] = kn[:, h * MLA_NOPE:(h + 1) * MLA_NOPE].astype(BF16)
        k_ref[0, h, :, MLA_NOPE:MLA_NOPE + MLA_ROPE] = kr
        k_ref[0, h, :, MLA_NOPE + MLA_ROPE:] = jnp.zeros((tm, npad), BF16)
        vt_ref[0, h] = vt[h * MLA_V:(h + 1) * MLA_V].astype(BF16)


def _mla_prep(p_main, csn, g_q, w_uqt, g_kv, w_kn, w_vt, batch, seq):
    tm = min(512, seq)
    per_b = seq // tm
    row = lambda b, i: b * per_b + i
    full = lambda a: pl.BlockSpec(a.shape, lambda b, i: (0,) * a.ndim)
    return pl.pallas_call(
        _mla_prep_kernel,
        grid=(batch, per_b),
        in_specs=[
            pl.BlockSpec((tm, MLA_Q_RANK), lambda b, i: (row(b, i), E_CQ // MLA_Q_RANK)),
            pl.BlockSpec((tm, MLA_KV_RANK), lambda b, i: (row(b, i), E_CKV // MLA_KV_RANK)),
            pl.BlockSpec((tm, LANE), lambda b, i: (row(b, i), E_KR // LANE)),
            pl.BlockSpec((tm, LANE), lambda b, i: (row(b, i), 0)),
            full(g_q), full(w_uqt), full(g_kv), full(w_kn), full(w_vt),
        ],
        out_specs=[
            pl.BlockSpec((1, MLA_HEADS, MLA_QK_PAD, tm), lambda b, i: (b, 0, 0, i)),
            pl.BlockSpec((1, MLA_HEADS, tm, MLA_QK_PAD), lambda b, i: (b, 0, i, 0)),
            pl.BlockSpec((1, MLA_HEADS, MLA_V, tm), lambda b, i: (b, 0, 0, i)),
        ],
        out_shape=[
            jax.ShapeDtypeStruct((batch, MLA_HEADS, MLA_QK_PAD, seq), BF16),
            jax.ShapeDtypeStruct((batch, MLA_HEADS, seq, MLA_QK_PAD), BF16),
            jax.ShapeDtypeStruct((batch, MLA_HEADS, MLA_V, seq), BF16),
        ],
        compiler_params=_cparams(("arbitrary", "arbitrary")),
        name="mla_prep",
    )(p_main, p_main, p_main, csn, g_q, w_uqt, g_kv, w_kn, w_vt)


class _AttnGroup:
    def __init__(self, c, qt_ref, k_ref, vt_ref, z_ref, o_ref):
        self.seq = k_ref.shape[2]
        self.tk = min(MLA_TK, self.seq)
        self.sub = min(MLA_TQ, qt_ref.shape[3])
        self.cols = slice(c * self.sub, (c + 1) * self.sub)
        self.qt_ref, self.k_ref, self.vt_ref, self.z_ref, self.o_ref = qt_ref, k_ref, vt_ref, z_ref, o_ref

        self.nk = self.seq // self.tk
        self.s, self.m, self.acc = [], None, None

    def score_block(self, j):
        qt = self.qt_ref[0, 0, :, self.cols]
        s = _bdot(self.k_ref[0, 0, j * self.tk:(j + 1) * self.tk, :], qt)
        mj = jnp.max(s, axis=0, keepdims=True)
        self.s.append(s)
        self.m = mj if self.m is None else jnp.maximum(self.m, mj)

    def value_block(self, j):
        ones = jnp.ones((MLA_ONES_ROWS, self.tk), BF16)
        p = jnp.exp2(self.s[j] - self.m).astype(BF16)
        va = jnp.concatenate([self.vt_ref[0, 0, :, j * self.tk:(j + 1) * self.tk], ones], axis=0)
        oj = _bdot(va, p)
        self.acc = oj if self.acc is None else self.acc + oj
        if j == self.nk - 1:
            ot = self.acc[:MLA_V] / self.acc[MLA_V:MLA_V + 1]
            gate = _silu(self.z_ref[self.cols, :].astype(F32))
            self.o_ref[self.cols, :] = (ot.T * gate).astype(self.o_ref.dtype)

    def score_tasks(self):
        return [functools.partial(self.score_block, j) for j in range(self.nk)]

    def value_tasks(self):
        return [functools.partial(self.value_block, j) for j in range(self.nk)]


def _attn_groups(qt_ref, k_ref, vt_ref, z_ref, o_ref):
    n = qt_ref.shape[3] // min(MLA_TQ, qt_ref.shape[3])
    return [_AttnGroup(c, qt_ref, k_ref, vt_ref, z_ref, o_ref) for c in range(n)]


def _attn_tasks(groups):
    tasks = list(groups[0].score_tasks())
    for c in range(1, len(groups)):
        tasks += groups[c].score_tasks() + groups[c - 1].value_tasks()
    return tasks + groups[-1].value_tasks()


def _mla_attn_kernel(qt_ref, k_ref, vt_ref, z_ref, o_ref):
    for task in _attn_tasks(_attn_groups(qt_ref, k_ref, vt_ref, z_ref, o_ref)):
        task()


def _mla_attention(q_t, k_cat, v_t, p_main, batch, seq):
    tq = min(MLA_TQ * MLA_GROUPS, seq)
    nq = seq // tq
    return pl.pallas_call(
        _mla_attn_kernel,
        grid=(batch, MLA_HEADS, nq),
        in_specs=[
            pl.BlockSpec((1, 1, MLA_QK_PAD, tq), lambda b, h, i: (b, h, 0, i)),
            pl.BlockSpec((1, 1, seq, MLA_QK_PAD), lambda b, h, i: (b, h, 0, 0)),
            pl.BlockSpec((1, 1, MLA_V, seq), lambda b, h, i: (b, h, 0, 0)),
            pl.BlockSpec((tq, MLA_V), lambda b, h, i: (b * nq + i, E_MZ // MLA_V + h)),
        ],
        out_specs=pl.BlockSpec((tq, MLA_V), lambda b, h, i: (b * nq + i, h)),
        out_shape=jax.ShapeDtypeStruct((batch * seq, MLA_WIDTH), BF16),
        compiler_params=_cparams(("arbitrary", "arbitrary", "arbitrary")),
        name="mla_attention",
    )(q_t, k_cat, v_t, p_main)


def _even_out_kernel(x_ref, of_ref, ob_ref, gz_ref, om_ref, mod_ref, gn_ref, w_ref, gp_ref, o_ref, *, d):
    oa = of_ref[...].astype(F32) + ob_ref[...].astype(F32)
    gn = gn_ref[...]
    parts = [_rms(oa[:, h * GLA_DV:(h + 1) * GLA_DV], gn) for h in range(GLA_HEADS)]
    oa = jnp.concatenate(parts, axis=-1) * _silu(gz_ref[...].astype(F32))
    y = _bdot(oa.astype(BF16), w_ref[0:GLA_WIDTH, :]) + _bdot(om_ref[...], w_ref[GLA_WIDTH:, :])
    o_ref[...] = _postnorm_residual(x_ref[...], y, mod_ref, gp_ref, d)


def _even_out_proj(x2, o_f, o_b, p_main, o_mla, mod_l, g_norm, w_out, g_post, seq):
    t, d = x2.shape
    tm = min(512, seq)
    per_b = seq // tm
    return pl.pallas_call(
        functools.partial(_even_out_kernel, d=d),
        grid=(t // tm,),
        in_specs=[
            pl.BlockSpec((tm, d), lambda i: (i, 0)),
            pl.BlockSpec((tm, GLA_WIDTH), lambda i: (i, 0)),
            pl.BlockSpec((tm, GLA_WIDTH), lambda i: (i, 0)),
            pl.BlockSpec((tm, GLA_WIDTH), lambda i: (i, E_GZ // GLA_WIDTH)),
            pl.BlockSpec((tm, MLA_WIDTH), lambda i: (i, 0)),
            pl.BlockSpec((1, 1, 3 * d), lambda i: (i // per_b, 0, 0)),
            pl.BlockSpec((1, GLA_DV), lambda i: (0, 0)),
            pl.BlockSpec((GLA_WIDTH + MLA_WIDTH, d), lambda i: (0, 0)),
            pl.BlockSpec((1, d), lambda i: (0, 0)),
        ],
        out_specs=pl.BlockSpec((tm, d), lambda i: (i, 0)),
        out_shape=jax.ShapeDtypeStruct((t, d), F32),
        compiler_params=_cparams(("arbitrary",)),
        name="even_out_proj",
    )(x2, o_f, o_b, p_main, o_mla, mod_l, g_norm, w_out, g_post)


def _odd_kernel(x_ref, mod_ref, gpre_ref, win_ref, gln_ref, bln_ref, ws_ref, bs_ref, wout_ref, gpost_ref,
                o_ref, *, d, width):
    x = x_ref[...]
    tm = x.shape[0]
    hb = _prenorm(x, mod_ref, gpre_ref, d).astype(BF16)
    gw = width // SG_GROUPS

    ncb = width // ODD_COLS
    proj = lambda base, c: _bdot(hb, win_ref[:, base + c * ODD_COLS:base + (c + 1) * ODD_COLS])
    vp = [proj(width, c) for c in range(ncb)]
    up, v = [], []
    for c in range(ncb):
        up.append(proj(0, c))
        v.append(_gelu(vp[c]))
    v = jnp.concatenate(v, axis=-1)
    mu = jnp.mean(v, axis=-1, keepdims=True)
    vc = v - mu
    var = jnp.mean(vc * vc, axis=-1, keepdims=True)
    vn = ((vc * lax.rsqrt(var + EPS)) * gln_ref[...] + bln_ref[...]).astype(BF16)
    zp, u = [], []
    for c in range(ncb):
        zp.append(proj(2 * width, c))
        u.append(_gelu(up[c]))

    rows = []
    for c in range(tm // SG_CHUNK):
        cols = []
        for g in range(SG_GROUPS):
            blk = vn[c * SG_CHUNK:(c + 1) * SG_CHUNK, g * gw:(g + 1) * gw]
            cols.append(_bdot(ws_ref[g], blk) + bs_ref[:, g:g + 1])
        rows.append(jnp.concatenate(cols, axis=-1))
    mixed = jnp.concatenate(rows, axis=0) if len(rows) > 1 else rows[0]

    gated = [(u[c] * mixed[:, c * ODD_COLS:(c + 1) * ODD_COLS] * _silu(zp[c])).astype(BF16)
             for c in range(ncb)]
    y = _bdot(jnp.concatenate(gated, axis=-1), wout_ref[...])
    o_ref[...] = _postnorm_residual(x, y, mod_ref, gpost_ref, d)


def _odd_layer(x2, mod_l, g_pre, w_in, g_ln, b_ln, w_s, b_s_t, w_out, g_post, seq, layer):
    t, d = x2.shape
    width = w_out.shape[1]
    tm = min(ODD_ROWS, seq)
    per_b = seq // tm
    full = lambda a: pl.BlockSpec(a.shape, lambda i: (0,) * a.ndim, pipeline_mode=pl.Buffered(1))
    stacked = lambda a: pl.BlockSpec((None,) + a.shape[1:], lambda i: (layer,) + (0,) * (a.ndim - 1),
                                     pipeline_mode=pl.Buffered(1))
    return pl.pallas_call(
        functools.partial(_odd_kernel, d=d, width=width),
        grid=(t // tm,),
        in_specs=[
            pl.BlockSpec((tm, d), lambda i: (i, 0)),
            pl.BlockSpec((1, 1, 3 * d), lambda i: (i // per_b, 0, 0)),
            full(g_pre), stacked(w_in), full(g_ln), full(b_ln), stacked(w_s), full(b_s_t), stacked(w_out),
            full(g_post),
        ],
        out_specs=pl.BlockSpec((tm, d), lambda i: (i, 0)),
        out_shape=jax.ShapeDtypeStruct((t, d), F32),
        compiler_params=_cparams(("arbitrary",)),
        name="odd_sgmlp",
    )(x2, mod_l, g_pre, w_in, g_ln, b_ln, w_s, b_s_t, w_out, g_post)


def _swap_halves(w):
    half = w.shape[-1] // 2
    return jnp.concatenate([w[..., half:], w[..., :half]], axis=-1)


def _even_in_weight(w):
    gq, gk, gv = w[:, 0:256], w[:, 256:512], w[:, 512:1024]
    ga_f, ga_b = w[:, 1024:1040], w[:, 1040:1056]
    gz, cq, ckv, kr, mz = w[:, 1056:1568], w[:, 1568:1824], w[:, 1824:1952], w[:, 1952:2016], w[:, 2016:2528]
    pad = jnp.zeros((w.shape[0], E_GA - 2 * GLA_GATE_RANK), w.dtype)
    return jnp.concatenate([gq, gk, gv, gz, mz, cq, ckv, kr, _swap_halves(kr), ga_f, ga_b, pad],
                           axis=-1).astype(BF16)


def _uq_weight(w):
    w3 = w.reshape(w.shape[0], MLA_HEADS, MLA_NOPE + MLA_ROPE)
    nope = w3[:, :, :MLA_NOPE].reshape(w.shape[0], -1)
    rope = w3[:, :, MLA_NOPE:]
    return jnp.concatenate([nope, rope.reshape(w.shape[0], -1), _swap_halves(rope).reshape(w.shape[0], -1)],
                           axis=-1).astype(BF16).T


def _ukv_weights(w):
    w3 = w.reshape(w.shape[0], MLA_HEADS, MLA_NOPE + MLA_V)
    w_kn = w3[:, :, :MLA_NOPE].reshape(w.shape[0], -1)
    w_v = w3[:, :, MLA_NOPE:].reshape(w.shape[0], -1)
    return w_kn.astype(BF16), w_v.astype(BF16).T


def _gate_weight(w, offset):
    out = jnp.zeros((E_GA, w.shape[1]), F32).at[offset:offset + w.shape[0]].set(w)
    hi, lo = _split_bf16(out)
    return jnp.concatenate([hi, hi, lo], axis=0)


def _rope_table(positions):
    half = MLA_ROPE // 2
    inv_freq = 1.0 / (ROPE_THETA ** (jnp.arange(0, MLA_ROPE, 2, dtype=F32) / MLA_ROPE))
    phase = jnp.concatenate([jnp.zeros((MLA_ROPE,), F32), jnp.full((half,), math.pi / 2, F32),
                             jnp.full((half,), -math.pi / 2, F32)])
    ang = positions.astype(F32).reshape(-1, 1) * jnp.tile(inv_freq, 4)
    return jnp.cos(ang + phase)


def kernel(x, c, positions, w_mod, b_mod, g_pre, g_post, w_in_e, gla_w_gate_f, gla_b_gate_f, gla_w_gate_b, gla_b_gate_b, gla_g_norm, mla_g_q, mla_w_uq, mla_g_kv, mla_w_ukv, w_out_e, w_in_o, sg_g_norm, sg_b_norm, sg_w_s, sg_b_s, w_out_o):
    batch, seq, d = x.shape
    depth = w_mod.shape[0]
    mod = _modulation(c, w_mod, b_mod)
    csn = _rope_table(positions)
    w_in_o_b, w_s_b, w_out_o_b = w_in_o.astype(BF16), sg_w_s.astype(BF16), w_out_o.astype(BF16)
    x2 = x.reshape(batch * seq, d)
    row = lambda a: a.reshape(1, -1)
    for l in range(depth):
        mod_l = mod[l][:, None, :]
        if l % 2 == 0:
            e = l // 2
            p_main, p_ga = _even_in_proj(x2, mod_l, row(g_pre[l]), _even_in_weight(w_in_e[e]), seq)
            w_kn, w_vt = _ukv_weights(mla_w_ukv[e])
            q_t, k_cat, v_t = _mla_prep(p_main, csn, row(mla_g_q[e]), _uq_weight(mla_w_uq[e]),
                                        row(mla_g_kv[e]), w_kn, w_vt, batch, seq)
            o_f, o_b = _gla(p_main, p_ga,
                            _gate_weight(gla_w_gate_f[e], 0), row(gla_b_gate_f[e]),
                            _gate_weight(gla_w_gate_b[e], GLA_GATE_RANK), row(gla_b_gate_b[e]),
                            batch, seq)
            o_mla = _mla_attention(q_t, k_cat, v_t, p_main, batch, seq)
            x2 = _even_out_proj(x2, o_f, o_b, p_main, o_mla, mod_l, row(gla_g_norm[e]),
                                w_out_e[e].astype(BF16), row(g_post[l]), seq)
        else:
            o = l // 2
            x2 = _odd_layer(x2, mod_l, row(g_pre[l]), w_in_o_b, row(sg_g_norm[o]), row(sg_b_norm[o]),
                            w_s_b, sg_b_s[o].T, w_out_o_b, row(g_post[l]), seq, o)
    return x2.reshape(batch, seq, d)
```

```python
import functools
import math

import numpy as np
import jax
import jax.numpy as jnp
from jax import lax
from jax.experimental import pallas as pl
from jax.experimental.pallas import tpu as pltpu

F32 = jnp.float32
BF16 = jnp.bfloat16

EPS = 1e-6
GLA_HEADS = 4
GLA_DK = 64
GLA_DV = 128
GLA_GATE_RANK = 16
GLA_TAU = 16.0
GLA_CHUNK = 64
GLA_QK = GLA_HEADS * GLA_DK
GLA_WIDTH = GLA_HEADS * GLA_DV
MLA_HEADS = 4
MLA_Q_RANK = 256
MLA_KV_RANK = 128
MLA_NOPE = 128
MLA_ROPE = 64
MLA_V = 128
MLA_WIDTH = MLA_HEADS * MLA_V
MLA_QK_PAD = 256
MLA_TQ = 512
MLA_GROUPS = 8
MLA_HEADS_PER_STEP = 2
MLA_TK = 512
MLA_ONES_ROWS = 16
ROPE_THETA = 10000.0
SG_GROUPS = 8
SG_CHUNK = 128
EVEN_IN_ROWS = 1024
EVEN_AUX_ROWS = 1024
ODD_COLS = 512
ODD_ROWS = 512
ODD_TAIL_ROWS = 256

LANE = 128
VMEM_LIMIT = 56 * 1024 * 1024

E_GQ, E_GK, E_GV, E_GZ, E_MZ, E_CQ, E_CKV, E_KR = 0, 256, 512, 1024, 1536, 2048, 2304, 2432
E_MAIN = 2560
E_GA = 128
E_TOTAL = E_MAIN + E_GA

GLA_LEVELS = (32, 16, 8, 4, 2, 1)
GLA_UNROLL = 2


def _cparams(sem, flags=None):
    return pltpu.CompilerParams(dimension_semantics=sem, vmem_limit_bytes=VMEM_LIMIT, flags=flags)


def _silu(x):
    return x * jax.nn.sigmoid(x)


def _gelu(x):
    return 0.5 * x * (1.0 + lax.erf(x * (2.0 ** -0.5)))


def _bdot(a, b):
    return jnp.dot(a, b, preferred_element_type=F32)


def _split_bf16(x):
    hi = x.astype(BF16)
    lo = (x - hi.astype(F32)).astype(BF16)
    return hi, lo


def _mod_kernel(c_ref, w_ref, b_ref, o_ref):
    ca = _silu(c_ref[...])
    o_ref[0] = jnp.dot(ca, w_ref[0], precision=lax.Precision.HIGHEST,
                       preferred_element_type=F32) + b_ref[0]


def _modulation(c, w_mod, b_mod):
    depth, d, n = w_mod.shape
    b = c.shape[0]
    tn = 1024
    return pl.pallas_call(
        _mod_kernel,
        grid=(depth, n // tn),
        in_specs=[
            pl.BlockSpec((b, d), lambda l, j: (0, 0)),
            pl.BlockSpec((1, d, tn), lambda l, j: (l, 0, j)),
            pl.BlockSpec((1, 1, tn), lambda l, j: (l, 0, j)),
        ],
        out_specs=pl.BlockSpec((1, b, tn), lambda l, j: (l, 0, j)),
        out_shape=jax.ShapeDtypeStruct((depth, b, n), F32),
        compiler_params=_cparams(("arbitrary", "arbitrary")),
        name="adaln_mod",
    )(c, w_mod, b_mod.reshape(depth, 1, n))


def _prenorm(x, mod_ref, g_ref, d):
    ms = jnp.mean(x * x, axis=-1, keepdims=True)
    shift = mod_ref[0, :, 0:d]
    scale = mod_ref[0, :, d:2 * d]
    return (x * lax.rsqrt(ms + EPS)) * g_ref[...] * (1.0 + scale) + shift


def _postnorm_residual(x, y, mod_ref, g_ref, d):
    ms = jnp.mean(y * y, axis=-1, keepdims=True)
    gate = mod_ref[0, :, 2 * d:3 * d]
    return x + gate * ((y * lax.rsqrt(ms + EPS)) * g_ref[...])


def _even_in_kernel(x_ref, mod_ref, g_ref, w_ref, o_ref, ga_ref, *, d):
    hb = _prenorm(x_ref[...], mod_ref, g_ref, d).astype(BF16)
    for c0 in range(0, E_MAIN, 512):
        o_ref[:, c0:c0 + 512] = _bdot(hb, w_ref[:, c0:c0 + 512]).astype(BF16)
    ga_ref[...] = _bdot(hb, w_ref[:, E_MAIN:E_TOTAL])


def _even_in_proj(x2, mod_l, g_pre, w_e, seq):
    t, d = x2.shape
    tm = min(EVEN_IN_ROWS, seq)
    per_b = seq // tm
    return pl.pallas_call(
        functools.partial(_even_in_kernel, d=d),
        grid=(t // tm,),
        in_specs=[
            pl.BlockSpec((tm, d), lambda i: (i, 0)),
            pl.BlockSpec((1, 1, 3 * d), lambda i: (i // per_b, 0, 0)),
            pl.BlockSpec((1, d), lambda i: (0, 0)),
            pl.BlockSpec((d, E_TOTAL), lambda i: (0, 0)),
        ],
        out_specs=[
            pl.BlockSpec((tm, E_MAIN), lambda i: (i, 0)),
            pl.BlockSpec((tm, E_GA), lambda i: (i, 0)),
        ],
        out_shape=[
            jax.ShapeDtypeStruct((t, E_MAIN), BF16),
            jax.ShapeDtypeStruct((t, E_GA), F32),
        ],
        compiler_params=_cparams(("arbitrary",)),
        name="even_in_proj",
    )(x2, mod_l, g_pre, w_e)


def _gla_constants():
    c = GLA_CHUNK
    idx = np.arange(c)
    nl = len(GLA_LEVELS)
    wf = np.zeros((nl + 2, c, c), np.float32)
    wb = np.zeros((nl + 2, c, c), np.float32)
    mf = np.zeros((nl + 1, c, c), np.float32)
    mb = np.zeros((nl, c, c), np.float32)
    for li, s in enumerate(GLA_LEVELS):
        mid = (idx // (2 * s)) * (2 * s) + s
        upper = (idx % (2 * s)) >= s
        for t in range(c):
            if upper[t]:
                wf[li, t, mid[t]:t + 1] = 1.0
                wb[li, t, mid[t]:t] = 1.0
            else:
                wf[li, t, t + 1:mid[t]] = 1.0
                wb[li, t, t:mid[t]] = 1.0
        same = (idx[:, None] // (2 * s)) == (idx[None, :] // (2 * s))
        mf[li] = same & upper[:, None] & (~upper[None, :])
        mb[li] = same & (~upper[:, None]) & upper[None, :]
    mf[nl] = np.eye(c)
    for t in range(c):
        wf[nl, t, :t + 1] = 1.0
        wf[nl + 1, t, t + 1:] = 1.0
        wb[nl, t, t:] = 1.0
        wb[nl + 1, t, :t] = 1.0
    tile = lambda m: np.tile(m, (1, 1, GLA_HEADS))
    return (wf.reshape(-1, c), wb.reshape(-1, c), tile(mf), tile(mb))


def _gla_gate(ga, w2_ref, b2_ref):
    a_hi, a_lo = _split_bf16(ga)
    x = _bdot(jnp.concatenate([a_hi, a_lo, a_hi], axis=1), w2_ref[...]) + b2_ref[...]
    return (jnp.minimum(x, 0.0) - jnp.log(1.0 + jnp.exp(-jnp.abs(x)))) * (1.0 / GLA_TAU)


def _block_diag(blocks):
    n = len(blocks)
    zero = jnp.zeros_like(blocks[0])
    return jnp.concatenate(
        [jnp.concatenate([blocks[i] if i == j else zero for j in range(n)], axis=1) for i in range(n)],
        axis=0)


_TN = (((0,), (0,)), ((), ()))
_NT = (((1,), (1,)), ((), ()))


def _pair_dot(lhs, blocks):
    outs = []
    for p in range(len(blocks) // 2):
        outs.append(_bdot(lhs[:, p * LANE:(p + 1) * LANE], _block_diag(blocks[2 * p:2 * p + 2])))
    return jnp.concatenate(outs, axis=1)


class _GlaDir:
    def __init__(self, qs, ks, vs, gs, wall_ref, mask_ref, head_mask, forward):
        self.q, self.k, self.v, self.g = qs, ks, vs, gs
        self.wall_ref, self.mask_ref, self.head_mask, self.forward = wall_ref, mask_ref, head_mask, forward
        self.n_levels = len(GLA_LEVELS) + (1 if forward else 0)
        self.n = len(qs)

    def exponents(self):
        g2 = [jnp.concatenate(_split_bf16(g), axis=0) for g in self.g]
        f = jnp.exp(_bdot(self.wall_ref[...], jnp.concatenate(g2, axis=1)))
        self.f = [f[:, i * GLA_QK:(i + 1) * GLA_QK] for i in range(self.n)]

    def level(self, li):
        c, nl = GLA_CHUNK, len(GLA_LEVELS)
        if li >= self.n_levels:
            return
        if li == 0:
            self.att = [None] * self.n
        for i, (q, k, f) in enumerate(zip(self.q, self.k, self.f)):
            if li < nl:
                fl = f[li * c:(li + 1) * c]
                qs, ks = (q * fl).astype(BF16), (k * fl).astype(BF16)
            else:
                qs, ks = q.astype(BF16), k.astype(BF16)
            kblk = jnp.where(self.head_mask, jnp.concatenate([ks] * GLA_HEADS, axis=0), jnp.zeros((), BF16))
            r = lax.dot_general(qs, kblk, _NT, preferred_element_type=F32) * self.mask_ref[li]
            self.att[i] = r if self.att[i] is None else self.att[i] + r

    def intra_values(self):
        self.o = [_pair_dot(att.astype(BF16), [v[:, h * GLA_DV:(h + 1) * GLA_DV] for h in range(GLA_HEADS)])
                  for att, v in zip(self.att, self.v)]

    def state_terms(self):
        c, nl = GLA_CHUNK, len(GLA_LEVELS)
        self.qb = [(q * f[nl * c:(nl + 1) * c]).astype(BF16) for q, f in zip(self.q, self.f)]
        self.upd, self.dec = [], []
        tot_row = nl * c + (c - 1 if self.forward else 0)
        for k, v, f in zip(self.k, self.v, self.f):
            ke = (k * f[(nl + 1) * c:(nl + 2) * c]).astype(BF16)
            upd = []
            for p in range(GLA_HEADS // 2):
                u = lax.dot_general(ke[:, p * LANE:(p + 1) * LANE], v[:, 2 * p * GLA_DV:(2 * p + 2) * GLA_DV],
                                    _TN, preferred_element_type=F32)
                upd += [u[:GLA_DK, :GLA_DV], u[GLA_DK:, GLA_DV:]]
            self.upd.append(upd)
            self.dec.append(jnp.broadcast_to(f[tot_row:tot_row + 1], (GLA_DV, GLA_QK)).T)

    def scan(self, state_ref):
        states = [state_ref[h] for h in range(GLA_HEADS)]
        outs = []
        for i in range(self.n):
            outs.append(self.o[i] + _pair_dot(self.qb[i], [s.astype(BF16) for s in states]))
            states = [states[h] * self.dec[i][h * GLA_DK:(h + 1) * GLA_DK] + self.upd[i][h]
                      for h in range(GLA_HEADS)]
        for h in range(GLA_HEADS):
            state_ref[h] = states[h]
        return outs


def _gla_head_mask():
    hc = GLA_HEADS * GLA_CHUNK
    return (lax.broadcasted_iota(jnp.int32, (hc, GLA_QK), 0) // GLA_CHUNK
            == lax.broadcasted_iota(jnp.int32, (hc, GLA_QK), 1) // GLA_DK)


def _gla_trip_stages(first_f, first_b, fwd_refs, bwd_refs, head_mask):
    qscale = GLA_DK ** -0.5
    c, u = GLA_CHUNK, GLA_UNROLL
    plan = {
        "f": ([slice((first_f + i) * c, (first_f + i + 1) * c) for i in range(u)],
              slice(first_f * c, (first_f + u) * c), [i * c for i in range(u)], fwd_refs, True),
        "b": ([slice((first_b - i) * c, (first_b - i + 1) * c) for i in range(u)],
              slice((first_b - u + 1) * c, (first_b + 1) * c), [(u - 1 - i) * c for i in range(u)],
              bwd_refs, False),
    }
    gates, dirs = {}, {}

    def gate(name):
        _, span, _, refs, _ = plan[name]
        gates[name] = _gla_gate(refs[3][span, :], refs[4], refs[5])

    def factors(name):
        rows, _, offs, refs, forward = plan[name]
        q_ref, k_ref, v_ref = refs[:3]
        data = [(q_ref[r, :].astype(F32) * qscale, k_ref[r, :].astype(F32), v_ref[r, :],
                 gates[name][o:o + c]) for r, o in zip(rows, offs)]
        dirs[name] = _GlaDir(*zip(*data), refs[6], refs[7], head_mask, forward)
        dirs[name].exponents()

    def finish(name):
        rows, _, _, refs, _ = plan[name]
        o_ref, state_ref = refs[8:]
        for r, o in zip(rows, dirs[name].scan(state_ref)):
            o_ref[r, :] = o.astype(o_ref.dtype)

    both = lambda fn: [functools.partial(fn, "f"), functools.partial(fn, "b")]
    levels = [functools.partial(lambda n, li: dirs[n].level(li), n, li)
              for li in range(len(GLA_LEVELS) + 1) for n in ("f", "b")]
    rest = levels + [lambda: dirs["f"].intra_values(), lambda: dirs["b"].intra_values(),
                     lambda: dirs["f"].state_terms(), lambda: dirs["b"].state_terms()] + both(finish)
    return both(gate), both(factors), rest


def _gla_kernel(qf_ref, kf_ref, vf_ref, gaf_ref, qb_ref, kb_ref, vb_ref, gab_ref,
                w2f_ref, b2f_ref, w2b_ref, b2b_ref, wallf_ref, wallb_ref, mf_ref, mb_ref,
                of_ref, ob_ref, st_f, st_b, *, n_chunks):
    @pl.when(pl.program_id(1) == 0)
    def _():
        st_f[...] = jnp.zeros(st_f.shape, st_f.dtype)
        st_b[...] = jnp.zeros(st_b.shape, st_b.dtype)

    head_mask = _gla_head_mask()
    fwd_refs = (qf_ref, kf_ref, vf_ref, gaf_ref, w2f_ref, b2f_ref, wallf_ref, mf_ref, of_ref, st_f)
    bwd_refs = (qb_ref, kb_ref, vb_ref, gab_ref, w2b_ref, b2b_ref, wallb_ref, mb_ref, ob_ref, st_b)

    trips = [_gla_trip_stages(t * GLA_UNROLL, n_chunks - 1 - t * GLA_UNROLL, fwd_refs, bwd_refs, head_mask)
             for t in range(n_chunks // GLA_UNROLL)]
    n_stages = 3
    for step in range(len(trips) + n_stages - 1):
        for stage in range(n_stages):
            t = step - stage
            if 0 <= t < len(trips):
                for task in trips[t][stage]:
                    task()


def _gla(p_main, p_ga, w2f, b2f, w2b, b2b, batch, seq):
    t = p_main.shape[0]
    rb = min(512, seq)
    nblk = seq // rb
    n_chunks = rb // GLA_CHUNK
    wallf, wallb, mf, mb = _gla_constants()
    wallf = jnp.asarray(np.concatenate([wallf, wallf], axis=1), BF16)
    wallb = jnp.asarray(np.concatenate([wallb, wallb], axis=1), BF16)
    mf, mb = jnp.asarray(mf, F32), jnp.asarray(mb, F32)

    fwd = lambda b, n: b * nblk + n
    bwd = lambda b, n: b * nblk + (nblk - 1 - n)

    def row_specs(rowf):
        return [
            pl.BlockSpec((rb, GLA_QK), lambda b, n: (rowf(b, n), E_GQ // GLA_QK)),
            pl.BlockSpec((rb, GLA_QK), lambda b, n: (rowf(b, n), E_GK // GLA_QK)),
            pl.BlockSpec((rb, GLA_WIDTH), lambda b, n: (rowf(b, n), E_GV // GLA_WIDTH)),
            pl.BlockSpec((rb, E_GA), lambda b, n: (rowf(b, n), 0)),
        ]

    full = lambda a: pl.BlockSpec(a.shape, lambda b, n: (0,) * a.ndim)
    consts = (w2f, b2f, w2b, b2b, wallf, wallb, mf, mb)
    return pl.pallas_call(
        functools.partial(_gla_kernel, n_chunks=n_chunks),
        grid=(batch, nblk),
        in_specs=row_specs(fwd) + row_specs(bwd) + [full(a) for a in consts],
        out_specs=[
            pl.BlockSpec((rb, GLA_WIDTH), lambda b, n: (fwd(b, n), 0)),
            pl.BlockSpec((rb, GLA_WIDTH), lambda b, n: (bwd(b, n), 0)),
        ],
        out_shape=[jax.ShapeDtypeStruct((t, GLA_WIDTH), BF16)] * 2,
        scratch_shapes=[
            pltpu.VMEM((GLA_HEADS, GLA_DK, GLA_DV), F32), pltpu.VMEM((GLA_HEADS, GLA_DK, GLA_DV), F32),
        ],
        compiler_params=_cparams(("arbitrary", "arbitrary")),
        name="gla_bidir",
    )(p_main, p_main, p_main, p_ga, p_main, p_main, p_main, p_ga, *consts)


def _rms(x, g):
    return (x * lax.rsqrt(jnp.mean(x * x, axis=-1, keepdims=True) + EPS)) * g


def _mla_prep_kernel(cq_ref, ckv_ref, kr_ref, csn_ref, gq_ref, wqt_ref, gkv_ref, wkn_ref,
                     wvt_ref, qt_ref, k_ref, vt_ref):
    scale = (MLA_NOPE + MLA_ROPE) ** -0.5 * math.log2(math.e)
    hw = MLA_HEADS * MLA_NOPE
    hr = MLA_HEADS * MLA_ROPE
    tm = cq_ref.shape[0]
    cqn =_rms(cq_ref[...].astype(F32), gq_ref[...]).astype(BF16)
    qt = lax.dot_general(wqt_ref[...], cqn, _NT, preferred_element_type=F32)
    csn = csn_ref[...]
    csn_t = csn.T
    cs2 = jnp.concatenate([csn_t[:MLA_ROPE]] * MLA_HEADS, axis=0)
    sn2 = jnp.concatenate([csn_t[MLA_ROPE:]] * MLA_HEADS, axis=0)
    qr = (qt[hw:hw + hr] * cs2 + qt[hw + hr:hw + 2 * hr] * sn2) * scale
    kvn = _rms(ckv_ref[...].astype(F32), gkv_ref[...]).astype(BF16)
    kn = _bdot(kvn, wkn_ref[...])
    vt = lax.dot_general(wvt_ref[...], kvn, _NT, preferred_element_type=F32)
    kt = kr_ref[...].astype(F32) * csn
    kr = (kt[:, :MLA_ROPE] + kt[:, MLA_ROPE:]).astype(BF16)
    npad = MLA_QK_PAD - MLA_NOPE - MLA_ROPE
    for h in range(MLA_HEADS):
        qt_ref[0, h, 0:MLA_NOPE, :] = (qt[h * MLA_NOPE:(h + 1) * MLA_NOPE] * scale).astype(BF16)
        qt_ref[0, h, MLA_NOPE:MLA_NOPE + MLA_ROPE, :] = qr[h * MLA_ROPE:(h + 1) * MLA_ROPE].astype(BF16)
        qt_ref[0, h, MLA_NOPE + MLA_ROPE:, :] = jnp.zeros((npad, tm), BF16)
        k_ref[0, h, :, 0:MLA_NOPE] = kn[:, h * MLA_NOPE:(h + 1) * MLA_NOPE].astype(BF16)
        k_ref[0, h, :, MLA_NOPE:MLA_NOPE + MLA_ROPE] = kr
        k_ref[0, h, :, MLA_NOPE + MLA_ROPE:] = jnp.zeros((tm, npad), BF16)
        vt_ref[0, h] = vt[h * MLA_V:(h + 1) * MLA_V].astype(BF16)


def _mla_prep(p_main, csn, g_q, w_uqt, g_kv, w_kn, w_vt, batch, seq):
    tm = min(EVEN_AUX_ROWS, seq)
    per_b = seq // tm
    row = lambda b, i: b * per_b + i
    full = lambda a: pl.BlockSpec(a.shape, lambda b, i: (0,) * a.ndim)
    return pl.pallas_call(
        _mla_prep_kernel,
        grid=(batch, per_b),
        in_specs=[
            pl.BlockSpec((tm, MLA_Q_RANK), lambda b, i: (row(b, i), E_CQ // MLA_Q_RANK)),
            pl.BlockSpec((tm, MLA_KV_RANK), lambda b, i: (row(b, i), E_CKV // MLA_KV_RANK)),
            pl.BlockSpec((tm, LANE), lambda b, i: (row(b, i), E_KR // LANE)),
            pl.BlockSpec((tm, LANE), lambda b, i: (row(b, i), 0)),
            full(g_q), full(w_uqt), full(g_kv), full(w_kn), full(w_vt),
        ],
        out_specs=[
            pl.BlockSpec((1, MLA_HEADS, MLA_QK_PAD, tm), lambda b, i: (b, 0, 0, i)),
            pl.BlockSpec((1, MLA_HEADS, tm, MLA_QK_PAD), lambda b, i: (b, 0, i, 0)),
            pl.BlockSpec((1, MLA_HEADS, MLA_V, tm), lambda b, i: (b, 0, 0, i)),
        ],
        out_shape=[
            jax.ShapeDtypeStruct((batch, MLA_HEADS, MLA_QK_PAD, seq), BF16),
            jax.ShapeDtypeStruct((batch, MLA_HEADS, seq, MLA_QK_PAD), BF16),
            jax.ShapeDtypeStruct((batch, MLA_HEADS, MLA_V, seq), BF16),
        ],
        compiler_params=_cparams(("arbitrary", "arbitrary")),
        name="mla_prep",
    )(p_main, p_main, p_main, csn, g_q, w_uqt, g_kv, w_kn, w_vt)


class _AttnGroup:
    def __init__(self, hh, c, qt_ref, k_ref, vt_ref, z_ref, o_ref):
        self.seq = k_ref.shape[2]
        self.tk = min(MLA_TK, self.seq)
        self.sub = min(MLA_TQ, qt_ref.shape[3])
        self.hh = hh
        self.cols = slice(c * self.sub, (c + 1) * self.sub)
        self.gate_cols = slice(hh * MLA_V, (hh + 1) * MLA_V)
        self.qt_ref, self.k_ref, self.vt_ref, self.z_ref, self.o_ref = qt_ref, k_ref, vt_ref, z_ref, o_ref
        self.nk = self.seq // self.tk
        self.s, self.m, self.acc = [], None, None

    def score_block(self, j):
        qt = self.qt_ref[0, self.hh, :, self.cols]
        s = _bdot(self.k_ref[0, self.hh, j * self.tk:(j + 1) * self.tk, :], qt)
        mj = jnp.max(s, axis=0, keepdims=True)
        self.s.append(s)
        self.m = mj if self.m is None else jnp.maximum(self.m, mj)

    def value_block(self, j):
        ones = jnp.ones((MLA_ONES_ROWS, self.tk), BF16)
        p = jnp.exp2(self.s[j] - self.m).astype(BF16)
        va = jnp.concatenate([self.vt_ref[0, self.hh, :, j * self.tk:(j + 1) * self.tk], ones], axis=0)
        oj = _bdot(va, p)
        self.acc = oj if self.acc is None else self.acc + oj
        if j == self.nk - 1:
            ot = self.acc[:MLA_V] / self.acc[MLA_V:MLA_V + 1]
            gate = _silu(self.z_ref[self.cols, self.gate_cols].astype(F32))
            self.o_ref[self.cols, self.gate_cols] = (ot.T * gate).astype(self.o_ref.dtype)

    def score_tasks(self):
        return [functools.partial(self.score_block, j) for j in range(self.nk)]

    def value_tasks(self):
        return [functools.partial(self.value_block, j) for j in range(self.nk)]


def _mla_attn_kernel(qt_ref, k_ref, vt_ref, z_ref, o_ref):
    n_cols = qt_ref.shape[3] // min(MLA_TQ, qt_ref.shape[3])
    groups = [_AttnGroup(hh, c, qt_ref, k_ref, vt_ref, z_ref, o_ref)
              for hh in range(qt_ref.shape[1]) for c in range(n_cols)]
    tasks = list(groups[0].score_tasks())
    for g in range(1, len(groups)):
        tasks += groups[g].score_tasks() + groups[g - 1].value_tasks()
    for task in tasks + groups[-1].value_tasks():
        task()


def _mla_attention(q_t, k_cat, v_t, p_main, batch, seq):
    tq = min(MLA_TQ * MLA_GROUPS, seq)
    nq = seq // tq
    hb = MLA_HEADS_PER_STEP
    return pl.pallas_call(
        _mla_attn_kernel,
        grid=(batch, MLA_HEADS // hb, nq),
        in_specs=[
            pl.BlockSpec((1, hb, MLA_QK_PAD, tq), lambda b, h, i: (b, h, 0, i)),
            pl.BlockSpec((1, hb, seq, MLA_QK_PAD), lambda b, h, i: (b, h, 0, 0)),
            pl.BlockSpec((1, hb, MLA_V, seq), lambda b, h, i: (b, h, 0, 0)),
            pl.BlockSpec((tq, hb * MLA_V), lambda b, h, i: (b * nq + i, E_MZ // (hb * MLA_V) + h)),
        ],
        out_specs=pl.BlockSpec((tq, hb * MLA_V), lambda b, h, i: (b * nq + i, h)),
        out_shape=jax.ShapeDtypeStruct((batch * seq, MLA_WIDTH), BF16),
        compiler_params=_cparams(("arbitrary", "arbitrary", "arbitrary")),
        name="mla_attention",
    )(q_t, k_cat, v_t, p_main)


def _even_out_kernel(x_ref, of_ref, ob_ref, gz_ref, om_ref, mod_ref, gn_ref, w_ref, gp_ref, o_ref, *, d):
    oa = of_ref[...].astype(F32) + ob_ref[...].astype(F32)
    gn = gn_ref[...]
    parts = [_rms(oa[:, h * GLA_DV:(h + 1) * GLA_DV], gn) for h in range(GLA_HEADS)]
    oa = jnp.concatenate(parts, axis=-1) * _silu(gz_ref[...].astype(F32))
    y = _bdot(oa.astype(BF16), w_ref[0:GLA_WIDTH, :]) + _bdot(om_ref[...], w_ref[GLA_WIDTH:, :])
    o_ref[...] = _postnorm_residual(x_ref[...], y, mod_ref, gp_ref, d)


def _even_out_proj(x2, o_f, o_b, p_main, o_mla, mod_l, g_norm, w_out, g_post, seq):
    t, d = x2.shape
    tm = min(EVEN_AUX_ROWS, seq)
    per_b = seq // tm
    return pl.pallas_call(
        functools.partial(_even_out_kernel, d=d),
        grid=(t // tm,),
        in_specs=[
            pl.BlockSpec((tm, d), lambda i: (i, 0)),
            pl.BlockSpec((tm, GLA_WIDTH), lambda i: (i, 0)),
            pl.BlockSpec((tm, GLA_WIDTH), lambda i: (i, 0)),
            pl.BlockSpec((tm, GLA_WIDTH), lambda i: (i, E_GZ // GLA_WIDTH)),
            pl.BlockSpec((tm, MLA_WIDTH), lambda i: (i, 0)),
            pl.BlockSpec((1, 1, 3 * d), lambda i: (i // per_b, 0, 0)),
            pl.BlockSpec((1, GLA_DV), lambda i: (0, 0)),
            pl.BlockSpec((GLA_WIDTH + MLA_WIDTH, d), lambda i: (0, 0)),
            pl.BlockSpec((1, d), lambda i: (0, 0)),
        ],
        out_specs=pl.BlockSpec((tm, d), lambda i: (i, 0)),
        out_shape=jax.ShapeDtypeStruct((t, d), F32),
        compiler_params=_cparams(("arbitrary",)),
        name="even_out_proj",
    )(x2, o_f, o_b, p_main, o_mla, mod_l, g_norm, w_out, g_post)


def _odd_kernel(x_ref, mod_ref, gpre_ref, win_ref, gln_ref, bln_ref, ws_ref, bs_ref, wout_ref, gpost_ref,
                o_ref, *, d, width):
    x = x_ref[...]
    tm = x.shape[0]
    hb = _prenorm(x, mod_ref, gpre_ref, d).astype(BF16)
    gw = width // SG_GROUPS

    ncb = width // ODD_COLS
    proj = lambda base, c: _bdot(hb, win_ref[:, base + c * ODD_COLS:base + (c + 1) * ODD_COLS])
    vp = [proj(width, c) for c in range(ncb)]
    up, v = [], []
    for c in range(ncb):
        up.append(proj(0, c))
        v.append(_gelu(vp[c]))
    v = jnp.concatenate(v, axis=-1)
    zp, u = [], []
    for c in range(ncb):
        zp.append(proj(2 * width, c))
        u.append(_gelu(up[c]))

    tail = min(ODD_TAIL_ROWS, tm)
    for r0 in range(0, tm, tail):
        rs = slice(r0, r0 + tail)
        vh = v[rs]
        mu = jnp.mean(vh, axis=-1, keepdims=True)
        vc = vh - mu
        var = jnp.mean(vc * vc, axis=-1, keepdims=True)
        vn = ((vc * lax.rsqrt(var + EPS)) * gln_ref[...] + bln_ref[...]).astype(BF16)
        rows = []
        for c in range(tail // SG_CHUNK):
            cols = []
            for g in range(SG_GROUPS):
                blk = vn[c * SG_CHUNK:(c + 1) * SG_CHUNK, g * gw:(g + 1) * gw]
                cols.append(_bdot(ws_ref[g], blk) + bs_ref[:, g:g + 1])
            rows.append(jnp.concatenate(cols, axis=-1))
        mixed = jnp.concatenate(rows, axis=0) if len(rows) > 1 else rows[0]
        gated = [(u[c][rs] * mixed[:, c * ODD_COLS:(c + 1) * ODD_COLS] * _silu(zp[c][rs])).astype(BF16)
                 for c in range(ncb)]
        y = _bdot(jnp.concatenate(gated, axis=-1), wout_ref[...])
        o_ref[rs, :] = _postnorm_residual(x[rs], y, mod_ref, gpost_ref, d)


def _odd_layer(x2, mod_l, g_pre, w_in, g_ln, b_ln, w_s, b_s_t, w_out, g_post, seq, layer):
    t, d = x2.shape
    width = w_out.shape[1]
    tm = min(ODD_ROWS, seq)
    per_b = seq // tm
    full = lambda a: pl.BlockSpec(a.shape, lambda i: (0,) * a.ndim, pipeline_mode=pl.Buffered(1))
    stacked = lambda a: pl.BlockSpec((None,) + a.shape[1:], lambda i: (layer,) + (0,) * (a.ndim - 1),
                                     pipeline_mode=pl.Buffered(1))
    return pl.pallas_call(
        functools.partial(_odd_kernel, d=d, width=width),
        grid=(t // tm,),
        in_specs=[
            pl.BlockSpec((tm, d), lambda i: (i, 0)),
            pl.BlockSpec((1, 1, 3 * d), lambda i: (i // per_b, 0, 0)),
            full(g_pre), stacked(w_in), full(g_ln), full(b_ln), stacked(w_s), full(b_s_t), stacked(w_out),
            full(g_post),
        ],
        out_specs=pl.BlockSpec((tm, d), lambda i: (i, 0)),
        out_shape=jax.ShapeDtypeStruct((t, d), F32),
        compiler_params=_cparams(("arbitrary",)),
        name="odd_sgmlp",
    )(x2, mod_l, g_pre, w_in, g_ln, b_ln, w_s, b_s_t, w_out, g_post)


def _swap_halves(w):
    half = w.shape[-1] // 2
    return jnp.concatenate([w[..., half:], w[..., :half]], axis=-1)


def _even_in_weight(w):
    gq, gk, gv = w[:, 0:256], w[:, 256:512], w[:, 512:1024]
    ga_f, ga_b = w[:, 1024:1040], w[:, 1040:1056]
    gz, cq, ckv, kr, mz = w[:, 1056:1568], w[:, 1568:1824], w[:, 1824:1952], w[:, 1952:2016], w[:, 2016:2528]
    pad = jnp.zeros((w.shape[0], E_GA - 2 * GLA_GATE_RANK), w.dtype)
    return jnp.concatenate([gq, gk, gv, gz, mz, cq, ckv, kr, _swap_halves(kr), ga_f, ga_b, pad],
                           axis=-1).astype(BF16)


def _uq_weight(w):
    w3 = w.reshape(w.shape[0], MLA_HEADS, MLA_NOPE + MLA_ROPE)
    nope = w3[:, :, :MLA_NOPE].reshape(w.shape[0], -1)
    rope = w3[:, :, MLA_NOPE:]
    return jnp.concatenate([nope, rope.reshape(w.shape[0], -1), _swap_halves(rope).reshape(w.shape[0], -1)],
                           axis=-1).astype(BF16).T


def _ukv_weights(w):
    w3 = w.reshape(w.shape[0], MLA_HEADS, MLA_NOPE + MLA_V)
    w_kn = w3[:, :, :MLA_NOPE].reshape(w.shape[0], -1)
    w_v = w3[:, :, MLA_NOPE:].reshape(w.shape[0], -1)
    return w_kn.astype(BF16), w_v.astype(BF16).T


def _gate_weight(w, offset):
    out = jnp.zeros((E_GA, w.shape[1]), F32).at[offset:offset + w.shape[0]].set(w)
    hi, lo = _split_bf16(out)
    return jnp.concatenate([hi, hi, lo], axis=0)


def _rope_table(positions):
    half = MLA_ROPE // 2
    inv_freq = 1.0 / (ROPE_THETA ** (jnp.arange(0, MLA_ROPE, 2, dtype=F32) / MLA_ROPE))
    phase = jnp.concatenate([jnp.zeros((MLA_ROPE,), F32), jnp.full((half,), math.pi / 2, F32),
                             jnp.full((half,), -math.pi / 2, F32)])
    ang = positions.astype(F32).reshape(-1, 1) * jnp.tile(inv_freq, 4)
    return jnp.cos(ang + phase)


def kernel(x, c, positions, w_mod, b_mod, g_pre, g_post, w_in_e, gla_w_gate_f, gla_b_gate_f, gla_w_gate_b, gla_b_gate_b, gla_g_norm, mla_g_q, mla_w_uq, mla_g_kv, mla_w_ukv, w_out_e, w_in_o, sg_g_norm, sg_b_norm, sg_w_s, sg_b_s, w_out_o):
    batch, seq, d = x.shape
    depth = w_mod.shape[0]
    mod = _modulation(c, w_mod, b_mod)
    csn = _rope_table(positions)
    w_in_o_b, w_s_b, w_out_o_b = w_in_o.astype(BF16), sg_w_s.astype(BF16), w_out_o.astype(BF16)
    x2 = x.reshape(batch * seq, d)
    row = lambda a: a.reshape(1, -1)
    for l in range(depth):
        mod_l = mod[l][:, None, :]
        if l % 2 == 0:
            e = l // 2
            p_main, p_ga = _even_in_proj(x2, mod_l, row(g_pre[l]), _even_in_weight(w_in_e[e]), seq)
            w_kn, w_vt = _ukv_weights(mla_w_ukv[e])
            q_t, k_cat, v_t = _mla_prep(p_main, csn, row(mla_g_q[e]), _uq_weight(mla_w_uq[e]),
                                        row(mla_g_kv[e]), w_kn, w_vt, batch, seq)
            o_f, o_b = _gla(p_main, p_ga,
                            _gate_weight(gla_w_gate_f[e], 0), row(gla_b_gate_f[e]),
                            _gate_weight(gla_w_gate_b[e], GLA_GATE_RANK), row(gla_b_gate_b[e]),
                            batch, seq)
            o_mla = _mla_attention(q_t, k_cat, v_t, p_main, batch, seq)
            x2 = _even_out_proj(x2, o_f, o_b, p_main, o_mla, mod_l, row(gla_g_norm[e]),
                                w_out_e[e].astype(BF16), row(g_post[l]), seq)
        else:
            o = l // 2
            x2 = _odd_layer(x2, mod_l, row(g_pre[l]), w_in_o_b, row(sg_g_norm[o]), row(sg_b_norm[o]),
                            w_s_b, sg_b_s[o].T, w_out_o_b, row(g_post[l]), seq, o)
    return x2.reshape(batch, seq, d)
```

```python
import functools
import math

import numpy as np
import jax
import jax.numpy as jnp
from jax import lax
from jax.experimental import pallas as pl
from jax.experimental.pallas import tpu as pltpu

F32 = jnp.float32
BF16 = jnp.bfloat16

EPS = 1e-6
GLA_HEADS = 4
GLA_DK = 64
GLA_DV = 128
GLA_GATE_RANK = 16
GLA_TAU = 16.0
GLA_CHUNK = 64
GLA_QK = GLA_HEADS * GLA_DK
GLA_WIDTH = GLA_HEADS * GLA_DV
MLA_HEADS = 4
MLA_Q_RANK = 256
MLA_KV_RANK = 128
MLA_NOPE = 128
MLA_ROPE = 64
MLA_V = 128
MLA_WIDTH = MLA_HEADS * MLA_V
MLA_QK_PAD = 256
MLA_TQ = 512
MLA_GROUPS = 8
MLA_HEADS_PER_STEP = 1
MLA_TK = 512
MLA_ONES_ROWS = 16
ROPE_THETA = 10000.0
SG_GROUPS = 8
SG_CHUNK = 128
EVEN_IN_ROWS = 1024
EVEN_AUX_ROWS = 1024
ODD_COLS = 512
ODD_ROWS = 512
ODD_TAIL_ROWS = 256

LANE = 128
VMEM_LIMIT = 56 * 1024 * 1024

E_GQ, E_GK, E_GV, E_GZ, E_MZ, E_CQ, E_CKV, E_KR = 0, 256, 512, 1024, 1536, 2048, 2304, 2432
E_MAIN = 2560
E_GA = 128
E_TOTAL = E_MAIN + E_GA

GLA_LEVELS = (32, 16, 8, 4, 2, 1)
GLA_UNROLL = 2


def _cparams(sem, flags=None):
    return pltpu.CompilerParams(dimension_semantics=sem, vmem_limit_bytes=VMEM_LIMIT, flags=flags)


def _silu(x):
    return x * jax.nn.sigmoid(x)


def _gelu(x):
    return 0.5 * x * (1.0 + lax.erf(x * (2.0 ** -0.5)))


def _bdot(a, b):
    return jnp.dot(a, b, preferred_element_type=F32)


def _split_bf16(x):
    hi = x.astype(BF16)
    lo = (x - hi.astype(F32)).astype(BF16)
    return hi, lo


def _mod_kernel(c_ref, w_ref, b_ref, o_ref):
    ca = _silu(c_ref[...])
    o_ref[0] = jnp.dot(ca, w_ref[0], precision=lax.Precision.HIGHEST,
                       preferred_element_type=F32) + b_ref[0]


def _modulation(c, w_mod, b_mod):
    depth, d, n = w_mod.shape
    b = c.shape[0]
    tn = 1024
    return pl.pallas_call(
        _mod_kernel,
        grid=(depth, n // tn),
        in_specs=[
            pl.BlockSpec((b, d), lambda l, j: (0, 0)),
            pl.BlockSpec((1, d, tn), lambda l, j: (l, 0, j)),
            pl.BlockSpec((1, 1, tn), lambda l, j: (l, 0, j)),
        ],
        out_specs=pl.BlockSpec((1, b, tn), lambda l, j: (l, 0, j)),
        out_shape=jax.ShapeDtypeStruct((depth, b, n), F32),
        compiler_params=_cparams(("arbitrary", "arbitrary")),
        name="adaln_mod",
    )(c, w_mod, b_mod.reshape(depth, 1, n))


def _prenorm(x, mod_ref, g_ref, d):
    ms = jnp.mean(x * x, axis=-1, keepdims=True)
    shift = mod_ref[0, :, 0:d]
    scale = mod_ref[0, :, d:2 * d]
    return (x * lax.rsqrt(ms + EPS)) * g_ref[...] * (1.0 + scale) + shift


def _postnorm_residual(x, y, mod_ref, g_ref, d):
    ms = jnp.mean(y * y, axis=-1, keepdims=True)
    gate = mod_ref[0, :, 2 * d:3 * d]
    return x + gate * ((y * lax.rsqrt(ms + EPS)) * g_ref[...])


def _even_in_kernel(x_ref, mod_ref, g_ref, w_ref, o_ref, ga_ref, *, d):
    hb = _prenorm(x_ref[...], mod_ref, g_ref, d).astype(BF16)
    for c0 in range(0, E_MAIN, 512):
        o_ref[:, c0:c0 + 512] = _bdot(hb, w_ref[:, c0:c0 + 512]).astype(BF16)
    ga_ref[...] = _bdot(hb, w_ref[:, E_MAIN:E_TOTAL])


def _even_in_proj(x2, mod_l, g_pre, w_e, seq):
    t, d = x2.shape
    tm = min(EVEN_IN_ROWS, seq)
    per_b = seq // tm
    return pl.pallas_call(
        functools.partial(_even_in_kernel, d=d),
        grid=(t // tm,),
        in_specs=[
            pl.BlockSpec((tm, d), lambda i: (i, 0)),
            pl.BlockSpec((1, 1, 3 * d), lambda i: (i // per_b, 0, 0)),
            pl.BlockSpec((1, d), lambda i: (0, 0)),
            pl.BlockSpec((d, E_TOTAL), lambda i: (0, 0)),
        ],
        out_specs=[
            pl.BlockSpec((tm, E_MAIN), lambda i: (i, 0)),
            pl.BlockSpec((tm, E_GA), lambda i: (i, 0)),
        ],
        out_shape=[
            jax.ShapeDtypeStruct((t, E_MAIN), BF16),
            jax.ShapeDtypeStruct((t, E_GA), F32),
        ],
        compiler_params=_cparams(("arbitrary",)),
        name="even_in_proj",
    )(x2, mod_l, g_pre, w_e)


def _gla_constants():
    c = GLA_CHUNK
    idx = np.arange(c)
    nl = len(GLA_LEVELS)
    wf = np.zeros((nl + 2, c, c), np.float32)
    wb = np.zeros((nl + 2, c, c), np.float32)
    mf = np.zeros((nl + 1, c, c), np.float32)
    mb = np.zeros((nl, c, c), np.float32)
    for li, s in enumerate(GLA_LEVELS):
        mid = (idx // (2 * s)) * (2 * s) + s
        upper = (idx % (2 * s)) >= s
        for t in range(c):
            if upper[t]:
                wf[li, t, mid[t]:t + 1] = 1.0
                wb[li, t, mid[t]:t] = 1.0
            else:
                wf[li, t, t + 1:mid[t]] = 1.0
                wb[li, t, t:mid[t]] = 1.0
        same = (idx[:, None] // (2 * s)) == (idx[None, :] // (2 * s))
        mf[li] = same & upper[:, None] & (~upper[None, :])
        mb[li] = same & (~upper[:, None]) & upper[None, :]
    mf[nl] = np.eye(c)
    for t in range(c):
        wf[nl, t, :t + 1] = 1.0
        wf[nl + 1, t, t + 1:] = 1.0
        wb[nl, t, t:] = 1.0
        wb[nl + 1, t, :t] = 1.0
    tile = lambda m: np.tile(m, (1, 1, GLA_HEADS))
    return (wf.reshape(-1, c), wb.reshape(-1, c), tile(mf), tile(mb))


def _gla_gate(ga, w2_ref, b2_ref):
    a_hi, a_lo = _split_bf16(ga)
    x = _bdot(jnp.concatenate([a_hi, a_lo, a_hi], axis=1), w2_ref[...]) + b2_ref[...]
    return (jnp.minimum(x, 0.0) - jnp.log(1.0 + jnp.exp(-jnp.abs(x)))) * (1.0 / GLA_TAU)


def _block_diag(blocks):
    n = len(blocks)
    zero = jnp.zeros_like(blocks[0])
    return jnp.concatenate(
        [jnp.concatenate([blocks[i] if i == j else zero for j in range(n)], axis=1) for i in range(n)],
        axis=0)


_TN = (((0,), (0,)), ((), ()))
_NT = (((1,), (1,)), ((), ()))


def _pair_dot(lhs, blocks):
    outs = []
    for p in range(len(blocks) // 2):
        outs.append(_bdot(lhs[:, p * LANE:(p + 1) * LANE], _block_diag(blocks[2 * p:2 * p + 2])))
    return jnp.concatenate(outs, axis=1)


class _GlaDir:
    def __init__(self, qs, ks, vs, gs, wall_ref, mask_ref, head_mask, forward):
        self.q, self.k, self.v, self.g = qs, ks, vs, gs
        self.wall_ref, self.mask_ref, self.head_mask, self.forward = wall_ref, mask_ref, head_mask, forward
        self.n_levels = len(GLA_LEVELS) + (1 if forward else 0)
        self.n = len(qs)

    def exponents(self):
        g2 = [jnp.concatenate(_split_bf16(g), axis=0) for g in self.g]
        f = jnp.exp(_bdot(self.wall_ref[...], jnp.concatenate(g2, axis=1)))
        self.f = [f[:, i * GLA_QK:(i + 1) * GLA_QK] for i in range(self.n)]

    def level(self, li):
        c, nl = GLA_CHUNK, len(GLA_LEVELS)
        if li >= self.n_levels:
            return
        if li == 0:
            self.att = [None] * self.n
        for i, (q, k, f) in enumerate(zip(self.q, self.k, self.f)):
            if li < nl:
                fl = f[li * c:(li + 1) * c]
                qs, ks = (q * fl).astype(BF16), (k * fl).astype(BF16)
            else:
                qs, ks = q.astype(BF16), k.astype(BF16)
            kblk = jnp.where(self.head_mask, jnp.concatenate([ks] * GLA_HEADS, axis=0), jnp.zeros((), BF16))
            r = lax.dot_general(qs, kblk, _NT, preferred_element_type=F32) * self.mask_ref[li]
            self.att[i] = r if self.att[i] is None else self.att[i] + r

    def intra_values(self):
        self.o = [_pair_dot(att.astype(BF16), [v[:, h * GLA_DV:(h + 1) * GLA_DV] for h in range(GLA_HEADS)])
                  for att, v in zip(self.att, self.v)]

    def state_terms(self):
        c, nl = GLA_CHUNK, len(GLA_LEVELS)
        self.qb = [(q * f[nl * c:(nl + 1) * c]).astype(BF16) for q, f in zip(self.q, self.f)]
        self.upd, self.dec = [], []
        tot_row = nl * c + (c - 1 if self.forward else 0)
        for k, v, f in zip(self.k, self.v, self.f):
            ke = (k * f[(nl + 1) * c:(nl + 2) * c]).astype(BF16)
            upd = []
            for p in range(GLA_HEADS // 2):
                u = lax.dot_general(ke[:, p * LANE:(p + 1) * LANE], v[:, 2 * p * GLA_DV:(2 * p + 2) * GLA_DV],
                                    _TN, preferred_element_type=F32)
                upd += [u[:GLA_DK, :GLA_DV], u[GLA_DK:, GLA_DV:]]
            self.upd.append(upd)
            self.dec.append(jnp.broadcast_to(f[tot_row:tot_row + 1], (GLA_DV, GLA_QK)).T)

    def scan(self, state_ref):
        states = [state_ref[h] for h in range(GLA_HEADS)]
        outs = []
        for i in range(self.n):
            outs.append(self.o[i] + _pair_dot(self.qb[i], [s.astype(BF16) for s in states]))
            states = [states[h] * self.dec[i][h * GLA_DK:(h + 1) * GLA_DK] + self.upd[i][h]
                      for h in range(GLA_HEADS)]
        for h in range(GLA_HEADS):
            state_ref[h] = states[h]
        return outs


def _gla_head_mask():
    hc = GLA_HEADS * GLA_CHUNK
    return (lax.broadcasted_iota(jnp.int32, (hc, GLA_QK), 0) // GLA_CHUNK
            == lax.broadcasted_iota(jnp.int32, (hc, GLA_QK), 1) // GLA_DK)


def _gla_trip_stages(first_f, first_b, fwd_refs, bwd_refs, head_mask):
    qscale = GLA_DK ** -0.5
    c, u = GLA_CHUNK, GLA_UNROLL
    plan = {
        "f": ([slice((first_f + i) * c, (first_f + i + 1) * c) for i in range(u)],
              slice(first_f * c, (first_f + u) * c), [i * c for i in range(u)], fwd_refs, True),
        "b": ([slice((first_b - i) * c, (first_b - i + 1) * c) for i in range(u)],
              slice((first_b - u + 1) * c, (first_b + 1) * c), [(u - 1 - i) * c for i in range(u)],
              bwd_refs, False),
    }
    gates, dirs = {}, {}

    def gate(name):
        _, span, _, refs, _ = plan[name]
        gates[name] = _gla_gate(refs[3][span, :], refs[4], refs[5])

    def factors(name):
        rows, _, offs, refs, forward = plan[name]
        q_ref, k_ref, v_ref = refs[:3]
        data = [(q_ref[r, :].astype(F32) * qscale, k_ref[r, :].astype(F32), v_ref[r, :],
                 gates[name][o:o + c]) for r, o in zip(rows, offs)]
        dirs[name] = _GlaDir(*zip(*data), refs[6], refs[7], head_mask, forward)
        dirs[name].exponents()

    def finish(name):
        rows, _, _, refs, _ = plan[name]
        o_ref, state_ref = refs[8:]
        for r, o in zip(rows, dirs[name].scan(state_ref)):
            o_ref[r, :] = o.astype(o_ref.dtype)

    both = lambda fn: [functools.partial(fn, "f"), functools.partial(fn, "b")]
    levels = [functools.partial(lambda n, li: dirs[n].level(li), n, li)
              for li in range(len(GLA_LEVELS) + 1) for n in ("f", "b")]
    rest = levels + [lambda: dirs["f"].intra_values(), lambda: dirs["b"].intra_values(),
                     lambda: dirs["f"].state_terms(), lambda: dirs["b"].state_terms()] + both(finish)
    return both(gate), both(factors), rest


def _gla_kernel(qf_ref, kf_ref, vf_ref, gaf_ref, qb_ref, kb_ref, vb_ref, gab_ref,
                w2f_ref, b2f_ref, w2b_ref, b2b_ref, wallf_ref, wallb_ref, mf_ref, mb_ref,
                of_ref, ob_ref, st_f, st_b, *, n_chunks):
    @pl.when(pl.program_id(1) == 0)
    def _():
        st_f[...] = jnp.zeros(st_f.shape, st_f.dtype)
        st_b[...] = jnp.zeros(st_b.shape, st_b.dtype)

    head_mask = _gla_head_mask()
    fwd_refs = (qf_ref, kf_ref, vf_ref, gaf_ref, w2f_ref, b2f_ref, wallf_ref, mf_ref, of_ref, st_f)
    bwd_refs = (qb_ref, kb_ref, vb_ref, gab_ref, w2b_ref, b2b_ref, wallb_ref, mb_ref, ob_ref, st_b)

    trips = [_gla_trip_stages(t * GLA_UNROLL, n_chunks - 1 - t * GLA_UNROLL, fwd_refs, bwd_refs, head_mask)
             for t in range(n_chunks // GLA_UNROLL)]
    n_stages = 3
    for step in range(len(trips) + n_stages - 1):
        for stage in range(n_stages):
            t = step - stage
            if 0 <= t < len(trips):
                for task in trips[t][stage]:
                    task()


def _gla(p_main, p_ga, w2f, b2f, w2b, b2b, batch, seq):
    t = p_main.shape[0]
    rb = min(512, seq)
    nblk = seq // rb
    n_chunks = rb // GLA_CHUNK
    wallf, wallb, mf, mb = _gla_constants()
    wallf = jnp.asarray(np.concatenate([wallf, wallf], axis=1), BF16)
    wallb = jnp.asarray(np.concatenate([wallb, wallb], axis=1), BF16)
    mf, mb = jnp.asarray(mf, F32), jnp.asarray(mb, F32)

    fwd = lambda b, n: b * nblk + n
    bwd = lambda b, n: b * nblk + (nblk - 1 - n)

    def row_specs(rowf):
        return [
            pl.BlockSpec((rb, GLA_QK), lambda b, n: (rowf(b, n), E_GQ // GLA_QK)),
            pl.BlockSpec((rb, GLA_QK), lambda b, n: (rowf(b, n), E_GK // GLA_QK)),
            pl.BlockSpec((rb, GLA_WIDTH), lambda b, n: (rowf(b, n), E_GV // GLA_WIDTH)),
            pl.BlockSpec((rb, E_GA), lambda b, n: (rowf(b, n), 0)),
        ]

    full = lambda a: pl.BlockSpec(a.shape, lambda b, n: (0,) * a.ndim)
    consts = (w2f, b2f, w2b, b2b, wallf, wallb, mf, mb)
    return pl.pallas_call(
        functools.partial(_gla_kernel, n_chunks=n_chunks),
        grid=(batch, nblk),
        in_specs=row_specs(fwd) + row_specs(bwd) + [full(a) for a in consts],
        out_specs=[
            pl.BlockSpec((rb, GLA_WIDTH), lambda b, n: (fwd(b, n), 0)),
            pl.BlockSpec((rb, GLA_WIDTH), lambda b, n: (bwd(b, n), 0)),
        ],
        out_shape=[jax.ShapeDtypeStruct((t, GLA_WIDTH), BF16)] * 2,
        scratch_shapes=[
            pltpu.VMEM((GLA_HEADS, GLA_DK, GLA_DV), F32), pltpu.VMEM((GLA_HEADS, GLA_DK, GLA_DV), F32),
        ],
        compiler_params=_cparams(("arbitrary", "arbitrary")),
        name="gla_bidir",
    )(p_main, p_main, p_main, p_ga, p_main, p_main, p_main, p_ga, *consts)


def _rms(x, g):
    return (x * lax.rsqrt(jnp.mean(x * x, axis=-1, keepdims=True) + EPS)) * g


def _mla_prep_kernel(cq_ref, ckv_ref, kr_ref, csn_ref, gq_ref, wqt_ref, gkv_ref, wkn_ref,
                     wvt_ref, qt_ref, k_ref, vt_ref):
    scale = (MLA_NOPE + MLA_ROPE) ** -0.5 * math.log2(math.e)
    hw = MLA_HEADS * MLA_NOPE
    hr = MLA_HEADS * MLA_ROPE
    tm = cq_ref.shape[0]
    cqn =_rms(cq_ref[...].astype(F32), gq_ref[...]).astype(BF16)
    qt = lax.dot_general(wqt_ref[...], cqn, _NT, preferred_element_type=F32)
    csn = csn_ref[...]
    csn_t = csn.T
    cs2 = jnp.concatenate([csn_t[:MLA_ROPE]] * MLA_HEADS, axis=0)
    sn2 = jnp.concatenate([csn_t[MLA_ROPE:]] * MLA_HEADS, axis=0)
    qr = (qt[hw:hw + hr] * cs2 + qt[hw + hr:hw + 2 * hr] * sn2) * scale
    kvn = _rms(ckv_ref[...].astype(F32), gkv_ref[...]).astype(BF16)
    kn = _bdot(kvn, wkn_ref[...])
    vt = lax.dot_general(wvt_ref[...], kvn, _NT, preferred_element_type=F32)
    kt = kr_ref[...].astype(F32) * csn
    kr = (kt[:, :MLA_ROPE] + kt[:, MLA_ROPE:]).astype(BF16)
    npad = MLA_QK_PAD - MLA_NOPE - MLA_ROPE
    for h in range(MLA_HEADS):
        qt_ref[0, h, 0:MLA_NOPE, :] = (qt[h * MLA_NOPE:(h + 1) * MLA_NOPE] * scale).astype(BF16)
        qt_ref[0, h, MLA_NOPE:MLA_NOPE + MLA_ROPE, :] = qr[h * MLA_ROPE:(h + 1) * MLA_ROPE].astype(BF16)
        qt_ref[0, h, MLA_NOPE + MLA_ROPE:, :] = jnp.zeros((npad, tm), BF16)
        k_ref[0, h, :, 0:MLA_NOPE] = kn[:, h * MLA_NOPE:(h + 1) * MLA_NOPE].astype(BF16)
        k_ref[0, h, :, MLA_NOPE:MLA_NOPE + MLA_ROPE] = kr
        k_ref[0, h, :, MLA_NOPE + MLA_ROPE:] = jnp.zeros((tm, npad), BF16)
        vt_ref[0, h] = vt[h * MLA_V:(h + 1) * MLA_V].astype(BF16)


def _mla_prep(p_main, csn, g_q, w_uqt, g_kv, w_kn, w_vt, batch, seq):
    tm = min(EVEN_AUX_ROWS, seq)
    per_b = seq // tm
    row = lambda b, i: b * per_b + i
    full = lambda a: pl.BlockSpec(a.shape, lambda b, i: (0,) * a.ndim)
    return pl.pallas_call(
        _mla_prep_kernel,
        grid=(batch, per_b),
        in_specs=[
            pl.BlockSpec((tm, MLA_Q_RANK), lambda b, i: (row(b, i), E_CQ // MLA_Q_RANK)),
            pl.BlockSpec((tm, MLA_KV_RANK), lambda b, i: (row(b, i), E_CKV // MLA_KV_RANK)),
            pl.BlockSpec((tm, LANE), lambda b, i: (row(b, i), E_KR // LANE)),
            pl.BlockSpec((tm, LANE), lambda b, i: (row(b, i), 0)),
            full(g_q), full(w_uqt), full(g_kv), full(w_kn), full(w_vt),
        ],
        out_specs=[
            pl.BlockSpec((1, MLA_HEADS, MLA_QK_PAD, tm), lambda b, i: (b, 0, 0, i)),
            pl.BlockSpec((1, MLA_HEADS, tm, MLA_QK_PAD), lambda b, i: (b, 0, i, 0)),
            pl.BlockSpec((1, MLA_HEADS, MLA_V, tm), lambda b, i: (b, 0, 0, i)),
        ],
        out_shape=[
            jax.ShapeDtypeStruct((batch, MLA_HEADS, MLA_QK_PAD, seq), BF16),
            jax.ShapeDtypeStruct((batch, MLA_HEADS, seq, MLA_QK_PAD), BF16),
            jax.ShapeDtypeStruct((batch, MLA_HEADS, MLA_V, seq), BF16),
        ],
        compiler_params=_cparams(("arbitrary", "arbitrary")),
        name="mla_prep",
    )(p_main, p_main, p_main, csn, g_q, w_uqt, g_kv, w_kn, w_vt)


class _AttnGroup:
    def __init__(self, hh, c, qt_ref, k_ref, vt_ref, z_ref, o_ref):
        self.seq = k_ref.shape[2]
        self.tk = min(MLA_TK, self.seq)
        self.sub = min(MLA_TQ, qt_ref.shape[3])
        self.hh = hh
        self.cols = slice(c * self.sub, (c + 1) * self.sub)
        self.gate_cols = slice(hh * MLA_V, (hh + 1) * MLA_V)
        self.qt_ref, self.k_ref, self.vt_ref, self.z_ref, self.o_ref = qt_ref, k_ref, vt_ref, z_ref, o_ref
        self.nk = self.seq // self.tk
        self.s, self.m, self.acc = [], None, None

    def score_block(self, j):
        qt = self.qt_ref[0, self.hh, :, self.cols]
        s = _bdot(self.k_ref[0, self.hh, j * self.tk:(j + 1) * self.tk, :], qt)
        mj = jnp.max(s, axis=0, keepdims=True)
        self.s.append(s)
        self.m = mj if self.m is None else jnp.maximum(self.m, mj)

    def value_block(self, j):
        ones = jnp.ones((MLA_ONES_ROWS, self.tk), BF16)
        p = jnp.exp2(self.s[j] - self.m).astype(BF16)
        va = jnp.concatenate([self.vt_ref[0, self.hh, :, j * self.tk:(j + 1) * self.tk], ones], axis=0)
        oj = _bdot(va, p)
        self.acc = oj if self.acc is None else self.acc + oj
        if j == self.nk - 1:
            ot = self.acc[:MLA_V] / self.acc[MLA_V:MLA_V + 1]
            gate = _silu(self.z_ref[self.cols, self.gate_cols].astype(F32))
            self.o_ref[self.cols, self.gate_cols] = (ot.T * gate).astype(self.o_ref.dtype)

    def score_tasks(self):
        return [functools.partial(self.score_block, j) for j in range(self.nk)]

    def value_tasks(self):
        return [functools.partial(self.value_block, j) for j in range(self.nk)]


def _mla_attn_kernel(qt_ref, k_ref, vt_ref, z_ref, o_ref):
    n_cols = qt_ref.shape[3] // min(MLA_TQ, qt_ref.shape[3])
    groups = [_AttnGroup(hh, c, qt_ref, k_ref, vt_ref, z_ref, o_ref)
              for hh in range(qt_ref.shape[1]) for c in range(n_cols)]
    tasks = list(groups[0].score_tasks())
    for g in range(1, len(groups)):
        tasks += groups[g].score_tasks() + groups[g - 1].value_tasks()
    for task in tasks + groups[-1].value_tasks():
        task()


def _mla_attention(q_t, k_cat, v_t, p_main, batch, seq):
    tq = min(MLA_TQ * MLA_GROUPS, seq)
    nq = seq // tq
    hb = MLA_HEADS_PER_STEP
    return pl.pallas_call(
        _mla_attn_kernel,
        grid=(batch, MLA_HEADS // hb, nq),
        in_specs=[
            pl.BlockSpec((1, hb, MLA_QK_PAD, tq), lambda b, h, i: (b, h, 0, i)),
            pl.BlockSpec((1, hb, seq, MLA_QK_PAD), lambda b, h, i: (b, h, 0, 0)),
            pl.BlockSpec((1, hb, MLA_V, seq), lambda b, h, i: (b, h, 0, 0)),
            pl.BlockSpec((tq, hb * MLA_V), lambda b, h, i: (b * nq + i, E_MZ // (hb * MLA_V) + h)),
        ],
        out_specs=pl.BlockSpec((tq, hb * MLA_V), lambda b, h, i: (b * nq + i, h)),
        out_shape=jax.ShapeDtypeStruct((batch * seq, MLA_WIDTH), BF16),
        compiler_params=_cparams(("arbitrary", "arbitrary", "arbitrary")),
        name="mla_attention",
    )(q_t, k_cat, v_t, p_main)


def _even_out_value(x, of_ref, ob_ref, gz_ref, om_ref, mod_ref, gn_ref, w_ref, gp_ref, d):
    oa = of_ref[...].astype(F32) + ob_ref[...].astype(F32)
    gn = gn_ref[...]
    parts = [_rms(oa[:, h * GLA_DV:(h + 1) * GLA_DV], gn) for h in range(GLA_HEADS)]
    oa = jnp.concatenate(parts, axis=-1) * _silu(gz_ref[...].astype(F32))
    y = _bdot(oa.astype(BF16), w_ref[0:GLA_WIDTH, :]) + _bdot(om_ref[...], w_ref[GLA_WIDTH:, :])
    return _postnorm_residual(x, y, mod_ref, gp_ref, d)


def _even_out_kernel(x_ref, of_ref, ob_ref, gz_ref, om_ref, mod_ref, gn_ref, w_ref, gp_ref, o_ref, *, d):
    o_ref[...] = _even_out_value(x_ref[...], of_ref, ob_ref, gz_ref, om_ref, mod_ref, gn_ref, w_ref, gp_ref, d)


def _even_out_proj(x2, o_f, o_b, p_main, o_mla, mod_l, g_norm, w_out, g_post, seq):
    t, d = x2.shape
    tm = min(EVEN_AUX_ROWS, seq)
    per_b = seq // tm
    return pl.pallas_call(
        functools.partial(_even_out_kernel, d=d),
        grid=(t // tm,),
        in_specs=[
            pl.BlockSpec((tm, d), lambda i: (i, 0)),
            pl.BlockSpec((tm, GLA_WIDTH), lambda i: (i, 0)),
            pl.BlockSpec((tm, GLA_WIDTH), lambda i: (i, 0)),
            pl.BlockSpec((tm, GLA_WIDTH), lambda i: (i, E_GZ // GLA_WIDTH)),
            pl.BlockSpec((tm, MLA_WIDTH), lambda i: (i, 0)),
            pl.BlockSpec((1, 1, 3 * d), lambda i: (i // per_b, 0, 0)),
            pl.BlockSpec((1, GLA_DV), lambda i: (0, 0)),
            pl.BlockSpec((GLA_WIDTH + MLA_WIDTH, d), lambda i: (0, 0)),
            pl.BlockSpec((1, d), lambda i: (0, 0)),
        ],
        out_specs=pl.BlockSpec((tm, d), lambda i: (i, 0)),
        out_shape=jax.ShapeDtypeStruct((t, d), F32),
        compiler_params=_cparams(("arbitrary",)),
        name="even_out_proj",
    )(x2, o_f, o_b, p_main, o_mla, mod_l, g_norm, w_out, g_post)


def _odd_kernel(x_ref, mod_ref, gpre_ref, win_ref, gln_ref, bln_ref, ws_ref, bs_ref, wout_ref, gpost_ref,
                o_ref, *, d, width):
    _odd_body(x_ref[...], mod_ref, gpre_ref, win_ref, gln_ref, bln_ref, ws_ref, bs_ref, wout_ref, gpost_ref,
              o_ref, d, width)


def _even_out_odd_kernel(x_ref, of_ref, ob_ref, gz_ref, om_ref, emod_ref, gn_ref, we_ref, egp_ref,
                         mod_ref, gpre_ref, win_ref, gln_ref, bln_ref, ws_ref, bs_ref, wout_ref, gpost_ref,
                         o_ref, *, d, width):
    x = _even_out_value(x_ref[...], of_ref, ob_ref, gz_ref, om_ref, emod_ref, gn_ref, we_ref, egp_ref, d)
    _odd_body(x, mod_ref, gpre_ref, win_ref, gln_ref, bln_ref, ws_ref, bs_ref, wout_ref, gpost_ref,
              o_ref, d, width)


def _odd_body(x, mod_ref, gpre_ref, win_ref, gln_ref, bln_ref, ws_ref, bs_ref, wout_ref, gpost_ref,
              o_ref, d, width):
    tm = x.shape[0]
    hb = _prenorm(x, mod_ref, gpre_ref, d).astype(BF16)
    gw = width // SG_GROUPS

    ncb = width // ODD_COLS
    proj = lambda base, c: _bdot(hb, win_ref[:, base + c * ODD_COLS:base + (c + 1) * ODD_COLS])
    vp = [proj(width, c) for c in range(ncb)]
    up, v = [], []
    for c in range(ncb):
        up.append(proj(0, c))
        v.append(_gelu(vp[c]))
    v = jnp.concatenate(v, axis=-1)
    zp, u = [], []
    for c in range(ncb):
        zp.append(proj(2 * width, c))
        u.append(_gelu(up[c]))

    tail = min(ODD_TAIL_ROWS, tm)
    for r0 in range(0, tm, tail):
        rs = slice(r0, r0 + tail)
        vh = v[rs]
        mu = jnp.mean(vh, axis=-1, keepdims=True)
        vc = vh - mu
        var = jnp.mean(vc * vc, axis=-1, keepdims=True)
        vn = ((vc * lax.rsqrt(var + EPS)) * gln_ref[...] + bln_ref[...]).astype(BF16)
        rows = []
        for c in range(tail // SG_CHUNK):
            cols = []
            for g in range(SG_GROUPS):
                blk = vn[c * SG_CHUNK:(c + 1) * SG_CHUNK, g * gw:(g + 1) * gw]
                cols.append(_bdot(ws_ref[g], blk) + bs_ref[:, g:g + 1])
            rows.append(jnp.concatenate(cols, axis=-1))
        mixed = jnp.concatenate(rows, axis=0) if len(rows) > 1 else rows[0]
        gated = [(u[c][rs] * mixed[:, c * ODD_COLS:(c + 1) * ODD_COLS] * _silu(zp[c][rs])).astype(BF16)
                 for c in range(ncb)]
        y = _bdot(jnp.concatenate(gated, axis=-1), wout_ref[...])
        o_ref[rs, :] = _postnorm_residual(x[rs], y, mod_ref, gpost_ref, d)


def _odd_layer(x2, mod_l, g_pre, w_in, g_ln, b_ln, w_s, b_s_t, w_out, g_post, seq, layer):
    t, d = x2.shape
    width = w_out.shape[1]
    tm = min(ODD_ROWS, seq)
    per_b = seq // tm
    full = lambda a: pl.BlockSpec(a.shape, lambda i: (0,) * a.ndim, pipeline_mode=pl.Buffered(1))
    stacked = lambda a: pl.BlockSpec((None,) + a.shape[1:], lambda i: (layer,) + (0,) * (a.ndim - 1),
                                     pipeline_mode=pl.Buffered(1))
    return pl.pallas_call(
        functools.partial(_odd_kernel, d=d, width=width),
        grid=(t // tm,),
        in_specs=[
            pl.BlockSpec((tm, d), lambda i: (i, 0)),
            pl.BlockSpec((1, 1, 3 * d), lambda i: (i // per_b, 0, 0)),
            full(g_pre), stacked(w_in), full(g_ln), full(b_ln), stacked(w_s), full(b_s_t), stacked(w_out),
            full(g_post),
        ],
        out_specs=pl.BlockSpec((tm, d), lambda i: (i, 0)),
        out_shape=jax.ShapeDtypeStruct((t, d), F32),
        compiler_params=_cparams(("arbitrary",)),
        name="odd_sgmlp",
    )(x2, mod_l, g_pre, w_in, g_ln, b_ln, w_s, b_s_t, w_out, g_post)


def _even_out_odd_layer(x2, o_f, o_b, p_main, o_mla, emod_l, g_norm, w_out_e, eg_post,
                        mod_l, g_pre, w_in, g_ln, b_ln, w_s, b_s_t, w_out, g_post, seq, layer):
    t, d = x2.shape
    width = w_out.shape[1]
    tm = min(ODD_ROWS, seq)
    per_b = seq // tm
    full = lambda a: pl.BlockSpec(a.shape, lambda i: (0,) * a.ndim, pipeline_mode=pl.Buffered(1))
    stacked = lambda a: pl.BlockSpec((None,) + a.shape[1:], lambda i: (layer,) + (0,) * (a.ndim - 1),
                                     pipeline_mode=pl.Buffered(1))
    mod_spec = pl.BlockSpec((1, 1, 3 * d), lambda i: (i // per_b, 0, 0))
    return pl.pallas_call(
        functools.partial(_even_out_odd_kernel, d=d, width=width),
        grid=(t // tm,),
        in_specs=[
            pl.BlockSpec((tm, d), lambda i: (i, 0)),
            pl.BlockSpec((tm, GLA_WIDTH), lambda i: (i, 0)),
            pl.BlockSpec((tm, GLA_WIDTH), lambda i: (i, 0)),
            pl.BlockSpec((tm, GLA_WIDTH), lambda i: (i, E_GZ // GLA_WIDTH)),
            pl.BlockSpec((tm, MLA_WIDTH), lambda i: (i, 0)),
            mod_spec, full(g_norm), full(w_out_e), full(eg_post),
            mod_spec, full(g_pre), stacked(w_in), full(g_ln), full(b_ln), stacked(w_s), full(b_s_t),
            stacked(w_out), full(g_post),
        ],
        out_specs=pl.BlockSpec((tm, d), lambda i: (i, 0)),
        out_shape=jax.ShapeDtypeStruct((t, d), F32),
        compiler_params=_cparams(("arbitrary",)),
        name="even_out_odd",
    )(x2, o_f, o_b, p_main, o_mla, emod_l, g_norm, w_out_e, eg_post,
      mod_l, g_pre, w_in, g_ln, b_ln, w_s, b_s_t, w_out, g_post)


def _swap_halves(w):
    half = w.shape[-1] // 2
    return jnp.concatenate([w[..., half:], w[..., :half]], axis=-1)


def _even_in_weight(w):
    gq, gk, gv = w[:, 0:256], w[:, 256:512], w[:, 512:1024]
    ga_f, ga_b = w[:, 1024:1040], w[:, 1040:1056]
    gz, cq, ckv, kr, mz = w[:, 1056:1568], w[:, 1568:1824], w[:, 1824:1952], w[:, 1952:2016], w[:, 2016:2528]
    pad = jnp.zeros((w.shape[0], E_GA - 2 * GLA_GATE_RANK), w.dtype)
    return jnp.concatenate([gq, gk, gv, gz, mz, cq, ckv, kr, _swap_halves(kr), ga_f, ga_b, pad],
                           axis=-1).astype(BF16)


def _uq_weight(w):
    w3 = w.reshape(w.shape[0], MLA_HEADS, MLA_NOPE + MLA_ROPE)
    nope = w3[:, :, :MLA_NOPE].reshape(w.shape[0], -1)
    rope = w3[:, :, MLA_NOPE:]
    return jnp.concatenate([nope, rope.reshape(w.shape[0], -1), _swap_halves(rope).reshape(w.shape[0], -1)],
                           axis=-1).astype(BF16).T


def _ukv_weights(w):
    w3 = w.reshape(w.shape[0], MLA_HEADS, MLA_NOPE + MLA_V)
    w_kn = w3[:, :, :MLA_NOPE].reshape(w.shape[0], -1)
    w_v = w3[:, :, MLA_NOPE:].reshape(w.shape[0], -1)
    return w_kn.astype(BF16), w_v.astype(BF16).T


def _gate_weight(w, offset):
    out = jnp.zeros((E_GA, w.shape[1]), F32).at[offset:offset + w.shape[0]].set(w)
    hi, lo = _split_bf16(out)
    return jnp.concatenate([hi, hi, lo], axis=0)


def _rope_table(positions):
    half = MLA_ROPE // 2
    inv_freq = 1.0 / (ROPE_THETA ** (jnp.arange(0, MLA_ROPE, 2, dtype=F32) / MLA_ROPE))
    phase = jnp.concatenate([jnp.zeros((MLA_ROPE,), F32), jnp.full((half,), math.pi / 2, F32),
                             jnp.full((half,), -math.pi / 2, F32)])
    ang = positions.astype(F32).reshape(-1, 1) * jnp.tile(inv_freq, 4)
    return jnp.cos(ang + phase)


def kernel(x, c, positions, w_mod, b_mod, g_pre, g_post, w_in_e, gla_w_gate_f, gla_b_gate_f, gla_w_gate_b, gla_b_gate_b, gla_g_norm, mla_g_q, mla_w_uq, mla_g_kv, mla_w_ukv, w_out_e, w_in_o, sg_g_norm, sg_b_norm, sg_w_s, sg_b_s, w_out_o):
    batch, seq, d = x.shape
    depth = w_mod.shape[0]
    mod = _modulation(c, w_mod, b_mod)
    csn = _rope_table(positions)
    w_in_o_b, w_s_b, w_out_o_b = w_in_o.astype(BF16), sg_w_s.astype(BF16), w_out_o.astype(BF16)
    x2 = x.reshape(batch * seq, d)
    row = lambda a: a.reshape(1, -1)
    fused_odd = set()
    for l in range(depth):
        mod_l = mod[l][:, None, :]
        if l in fused_odd:
            continue
        if l % 2 == 0:
            e = l // 2
            p_main, p_ga = _even_in_proj(x2, mod_l, row(g_pre[l]), _even_in_weight(w_in_e[e]), seq)
            w_kn, w_vt = _ukv_weights(mla_w_ukv[e])
            q_t, k_cat, v_t = _mla_prep(p_main, csn, row(mla_g_q[e]), _uq_weight(mla_w_uq[e]),
                                        row(mla_g_kv[e]), w_kn, w_vt, batch, seq)
            o_f, o_b = _gla(p_main, p_ga,
                            _gate_weight(gla_w_gate_f[e], 0), row(gla_b_gate_f[e]),
                            _gate_weight(gla_w_gate_b[e], GLA_GATE_RANK), row(gla_b_gate_b[e]),
                            batch, seq)
            o_mla = _mla_attention(q_t, k_cat, v_t, p_main, batch, seq)
            if l + 1 < depth:
                o = (l + 1) // 2
                x2 = _even_out_odd_layer(
                    x2, o_f, o_b, p_main, o_mla, mod_l, row(gla_g_norm[e]), w_out_e[e].astype(BF16),
                    row(g_post[l]), mod[l + 1][:, None, :], row(g_pre[l + 1]), w_in_o_b, row(sg_g_norm[o]),
                    row(sg_b_norm[o]), w_s_b, sg_b_s[o].T, w_out_o_b, row(g_post[l + 1]), seq, o)
                fused_odd.add(l + 1)
            else:
                x2 = _even_out_proj(x2, o_f, o_b, p_main, o_mla, mod_l, row(gla_g_norm[e]),
                                    w_out_e[e].astype(BF16), row(g_post[l]), seq)
        else:
            o = l // 2
            x2 = _odd_layer(x2, mod_l, row(g_pre[l]), w_in_o_b, row(sg_g_norm[o]), row(sg_b_norm[o]),
                            w_s_b, sg_b_s[o].T, w_out_o_b, row(g_post[l]), seq, o)
    return x2.reshape(batch, seq, d)
```

```python
import functools
import math

import numpy as np
import jax
import jax.numpy as jnp
from jax import lax
from jax.experimental import pallas as pl
from jax.experimental.pallas import tpu as pltpu

F32 = jnp.float32
BF16 = jnp.bfloat16

EPS = 1e-6
GLA_HEADS = 4
GLA_DK = 64
GLA_DV = 128
GLA_GATE_RANK = 16
GLA_TAU = 16.0
GLA_CHUNK = 64
GLA_QK = GLA_HEADS * GLA_DK
GLA_WIDTH = GLA_HEADS * GLA_DV
MLA_HEADS = 4
MLA_Q_RANK = 256
MLA_KV_RANK = 128
MLA_NOPE = 128
MLA_ROPE = 64
MLA_V = 128
MLA_WIDTH = MLA_HEADS * MLA_V
MLA_QK_PAD = 256
MLA_TQ = 512
MLA_GROUPS = 8
MLA_HEADS_PER_STEP = 1
MLA_TK = 512
MLA_ONES_ROWS = 16
ROPE_THETA = 10000.0
SG_GROUPS = 8
SG_CHUNK = 128
EVEN_IN_ROWS = 1024
EVEN_AUX_ROWS = 1024
ODD_COLS = 512
ODD_ROWS = 512
ODD_TAIL_ROWS = 256

LANE = 128
VMEM_LIMIT = 56 * 1024 * 1024

E_GQ, E_GK, E_GV, E_GZ, E_MZ, E_CQ, E_CKV, E_KR = 0, 256, 512, 1024, 1536, 2048, 2304, 2432
E_MAIN = 2560
E_GA = 128
E_TOTAL = E_MAIN + E_GA

GLA_LEVELS = (32, 16, 8, 4, 2, 1)
GLA_UNROLL = 2


def _cparams(sem, flags=None):
    return pltpu.CompilerParams(dimension_semantics=sem, vmem_limit_bytes=VMEM_LIMIT, flags=flags)


def _silu(x):
    return x * jax.nn.sigmoid(x)


def _gelu(x):
    return 0.5 * x * (1.0 + lax.erf(x * (2.0 ** -0.5)))


def _bdot(a, b):
    return jnp.dot(a, b, preferred_element_type=F32)


def _split_bf16(x):
    hi = x.astype(BF16)
    lo = (x - hi.astype(F32)).astype(BF16)
    return hi, lo


def _mod_kernel(c_ref, w_ref, b_ref, o_ref):
    ca = _silu(c_ref[...])
    o_ref[0] = jnp.dot(ca, w_ref[0], precision=lax.Precision.HIGHEST,
                       preferred_element_type=F32) + b_ref[0]


def _modulation(c, w_mod, b_mod):
    depth, d, n = w_mod.shape
    b = c.shape[0]
    tn = 1024
    return pl.pallas_call(
        _mod_kernel,
        grid=(depth, n // tn),
        in_specs=[
            pl.BlockSpec((b, d), lambda l, j: (0, 0)),
            pl.BlockSpec((1, d, tn), lambda l, j: (l, 0, j)),
            pl.BlockSpec((1, 1, tn), lambda l, j: (l, 0, j)),
        ],
        out_specs=pl.BlockSpec((1, b, tn), lambda l, j: (l, 0, j)),
        out_shape=jax.ShapeDtypeStruct((depth, b, n), F32),
        compiler_params=_cparams(("arbitrary", "arbitrary")),
        name="adaln_mod",
    )(c, w_mod, b_mod.reshape(depth, 1, n))


def _prenorm(x, mod_ref, g_ref, d):
    ms = jnp.mean(x * x, axis=-1, keepdims=True)
    shift = mod_ref[0, :, 0:d]
    scale = mod_ref[0, :, d:2 * d]
    return (x * lax.rsqrt(ms + EPS)) * g_ref[...] * (1.0 + scale) + shift


def _postnorm_residual(x, y, mod_ref, g_ref, d):
    ms = jnp.mean(y * y, axis=-1, keepdims=True)
    gate = mod_ref[0, :, 2 * d:3 * d]
    return x + gate * ((y * lax.rsqrt(ms + EPS)) * g_ref[...])


def _even_in_kernel(x_ref, mod_ref, g_ref, w_ref, o_ref, ga_ref, *, d):
    hb = _prenorm(x_ref[...], mod_ref, g_ref, d).astype(BF16)
    for c0 in range(0, E_MAIN, 512):
        o_ref[:, c0:c0 + 512] = _bdot(hb, w_ref[:, c0:c0 + 512]).astype(BF16)
    ga_ref[...] = _bdot(hb, w_ref[:, E_MAIN:E_TOTAL])


def _even_in_proj(x2, mod_l, g_pre, w_e, seq):
    t, d = x2.shape
    tm = min(EVEN_IN_ROWS, seq)
    per_b = seq // tm
    return pl.pallas_call(
        functools.partial(_even_in_kernel, d=d),
        grid=(t // tm,),
        in_specs=[
            pl.BlockSpec((tm, d), lambda i: (i, 0)),
            pl.BlockSpec((1, 1, 3 * d), lambda i: (i // per_b, 0, 0)),
            pl.BlockSpec((1, d), lambda i: (0, 0)),
            pl.BlockSpec((d, E_TOTAL), lambda i: (0, 0)),
        ],
        out_specs=[
            pl.BlockSpec((tm, E_MAIN), lambda i: (i, 0)),
            pl.BlockSpec((tm, E_GA), lambda i: (i, 0)),
        ],
        out_shape=[
            jax.ShapeDtypeStruct((t, E_MAIN), BF16),
            jax.ShapeDtypeStruct((t, E_GA), F32),
        ],
        compiler_params=_cparams(("arbitrary",)),
        name="even_in_proj",
    )(x2, mod_l, g_pre, w_e)


def _gla_constants():
    c = GLA_CHUNK
    idx = np.arange(c)
    nl = len(GLA_LEVELS)
    wf = np.zeros((nl + 2, c, c), np.float32)
    wb = np.zeros((nl + 2, c, c), np.float32)
    mf = np.zeros((nl + 1, c, c), np.float32)
    mb = np.zeros((nl, c, c), np.float32)
    for li, s in enumerate(GLA_LEVELS):
        mid = (idx // (2 * s)) * (2 * s) + s
        upper = (idx % (2 * s)) >= s
        for t in range(c):
            if upper[t]:
                wf[li, t, mid[t]:t + 1] = 1.0
                wb[li, t, mid[t]:t] = 1.0
            else:
                wf[li, t, t + 1:mid[t]] = 1.0
                wb[li, t, t:mid[t]] = 1.0
        same = (idx[:, None] // (2 * s)) == (idx[None, :] // (2 * s))
        mf[li] = same & upper[:, None] & (~upper[None, :])
        mb[li] = same & (~upper[:, None]) & upper[None, :]
    mf[nl] = np.eye(c)
    for t in range(c):
        wf[nl, t, :t + 1] = 1.0
        wf[nl + 1, t, t + 1:] = 1.0
        wb[nl, t, t:] = 1.0
        wb[nl + 1, t, :t] = 1.0
    tile = lambda m: np.tile(m, (1, 1, GLA_HEADS))
    return (wf.reshape(-1, c), wb.reshape(-1, c), tile(mf), tile(mb))


def _gla_gate(ga, w2_ref, b2_ref):
    a_hi, a_lo = _split_bf16(ga)
    x = _bdot(jnp.concatenate([a_hi, a_lo, a_hi], axis=1), w2_ref[...]) + b2_ref[...]
    return (jnp.minimum(x, 0.0) - jnp.log(1.0 + jnp.exp(-jnp.abs(x)))) * (1.0 / GLA_TAU)


def _block_diag(blocks):
    n = len(blocks)
    zero = jnp.zeros_like(blocks[0])
    return jnp.concatenate(
        [jnp.concatenate([blocks[i] if i == j else zero for j in range(n)], axis=1) for i in range(n)],
        axis=0)


_TN = (((0,), (0,)), ((), ()))
_NT = (((1,), (1,)), ((), ()))


def _pair_dot(lhs, blocks):
    outs = []
    for p in range(len(blocks) // 2):
        outs.append(_bdot(lhs[:, p * LANE:(p + 1) * LANE], _block_diag(blocks[2 * p:2 * p + 2])))
    return jnp.concatenate(outs, axis=1)


class _GlaDir:
    def __init__(self, qs, ks, vs, gs, wall_ref, mask_ref, head_mask, forward):
        self.q, self.k, self.v, self.g = qs, ks, vs, gs
        self.wall_ref, self.mask_ref, self.head_mask, self.forward = wall_ref, mask_ref, head_mask, forward
        self.n_levels = len(GLA_LEVELS) + (1 if forward else 0)
        self.n = len(qs)

    def exponents(self):
        g2 = [jnp.concatenate(_split_bf16(g), axis=0) for g in self.g]
        f = jnp.exp(_bdot(self.wall_ref[...], jnp.concatenate(g2, axis=1)))
        self.f = [f[:, i * GLA_QK:(i + 1) * GLA_QK] for i in range(self.n)]

    def level(self, li):
        c, nl = GLA_CHUNK, len(GLA_LEVELS)
        if li >= self.n_levels:
            return
        if li == 0:
            self.att = [None] * self.n
        for i, (q, k, f) in enumerate(zip(self.q, self.k, self.f)):
            if li < nl:
                fl = f[li * c:(li + 1) * c]
                qs, ks = (q * fl).astype(BF16), (k * fl).astype(BF16)
            else:
                qs, ks = q.astype(BF16), k.astype(BF16)
            kblk = jnp.where(self.head_mask, jnp.concatenate([ks] * GLA_HEADS, axis=0), jnp.zeros((), BF16))
            r = lax.dot_general(qs, kblk, _NT, preferred_element_type=F32) * self.mask_ref[li]
            self.att[i] = r if self.att[i] is None else self.att[i] + r

    def intra_values(self):
        self.o = [_pair_dot(att.astype(BF16), [v[:, h * GLA_DV:(h + 1) * GLA_DV] for h in range(GLA_HEADS)])
                  for att, v in zip(self.att, self.v)]

    def state_terms(self):
        c, nl = GLA_CHUNK, len(GLA_LEVELS)
        self.qb = [(q * f[nl * c:(nl + 1) * c]).astype(BF16) for q, f in zip(self.q, self.f)]
        self.upd, self.dec = [], []
        tot_row = nl * c + (c - 1 if self.forward else 0)
        for k, v, f in zip(self.k, self.v, self.f):
            ke = (k * f[(nl + 1) * c:(nl + 2) * c]).astype(BF16)
            upd = []
            for p in range(GLA_HEADS // 2):
                u = lax.dot_general(ke[:, p * LANE:(p + 1) * LANE], v[:, 2 * p * GLA_DV:(2 * p + 2) * GLA_DV],
                                    _TN, preferred_element_type=F32)
                upd += [u[:GLA_DK, :GLA_DV], u[GLA_DK:, GLA_DV:]]
            self.upd.append(upd)
            self.dec.append(jnp.broadcast_to(f[tot_row:tot_row + 1], (GLA_DV, GLA_QK)).T)

    def scan(self, state_ref):
        states = [state_ref[h] for h in range(GLA_HEADS)]
        outs = []
        for i in range(self.n):
            outs.append(self.o[i] + _pair_dot(self.qb[i], [s.astype(BF16) for s in states]))
            states = [states[h] * self.dec[i][h * GLA_DK:(h + 1) * GLA_DK] + self.upd[i][h]
                      for h in range(GLA_HEADS)]
        for h in range(GLA_HEADS):
            state_ref[h] = states[h]
        return outs


def _gla_head_mask():
    hc = GLA_HEADS * GLA_CHUNK
    return (lax.broadcasted_iota(jnp.int32, (hc, GLA_QK), 0) // GLA_CHUNK
            == lax.broadcasted_iota(jnp.int32, (hc, GLA_QK), 1) // GLA_DK)


def _gla_trip_stages(first_f, first_b, fwd_refs, bwd_refs, head_mask):
    qscale = GLA_DK ** -0.5
    c, u = GLA_CHUNK, GLA_UNROLL
    plan = {
        "f": ([slice((first_f + i) * c, (first_f + i + 1) * c) for i in range(u)],
              slice(first_f * c, (first_f + u) * c), [i * c for i in range(u)], fwd_refs, True),
        "b": ([slice((first_b - i) * c, (first_b - i + 1) * c) for i in range(u)],
              slice((first_b - u + 1) * c, (first_b + 1) * c), [(u - 1 - i) * c for i in range(u)],
              bwd_refs, False),
    }
    gates, dirs = {}, {}

    def gate(name):
        _, span, _, refs, _ = plan[name]
        gates[name] = _gla_gate(refs[3][span, :], refs[4], refs[5])

    def factors(name):
        rows, _, offs, refs, forward = plan[name]
        q_ref, k_ref, v_ref = refs[:3]
        data = [(q_ref[r, :].astype(F32) * qscale, k_ref[r, :].astype(F32), v_ref[r, :],
                 gates[name][o:o + c]) for r, o in zip(rows, offs)]
        dirs[name] = _GlaDir(*zip(*data), refs[6], refs[7], head_mask, forward)
        dirs[name].exponents()

    def finish(name):
        rows, _, _, refs, _ = plan[name]
        o_ref, state_ref = refs[8:]
        for r, o in zip(rows, dirs[name].scan(state_ref)):
            o_ref[r, :] = o.astype(o_ref.dtype)

    both = lambda fn: [functools.partial(fn, "f"), functools.partial(fn, "b")]
    levels = [functools.partial(lambda n, li: dirs[n].level(li), n, li)
              for li in range(len(GLA_LEVELS) + 1) for n in ("f", "b")]
    rest = levels + [lambda: dirs["f"].intra_values(), lambda: dirs["b"].intra_values(),
                     lambda: dirs["f"].state_terms(), lambda: dirs["b"].state_terms()] + both(finish)
    return both(gate), both(factors), rest


def _gla_kernel(qf_ref, kf_ref, vf_ref, gaf_ref, qb_ref, kb_ref, vb_ref, gab_ref,
                w2f_ref, b2f_ref, w2b_ref, b2b_ref, wallf_ref, wallb_ref, mf_ref, mb_ref,
                of_ref, ob_ref, st_f, st_b, *, n_chunks):
    @pl.when(pl.program_id(1) == 0)
    def _():
        st_f[...] = jnp.zeros(st_f.shape, st_f.dtype)
        st_b[...] = jnp.zeros(st_b.shape, st_b.dtype)

    head_mask = _gla_head_mask()
    fwd_refs = (qf_ref, kf_ref, vf_ref, gaf_ref, w2f_ref, b2f_ref, wallf_ref, mf_ref, of_ref, st_f)
    bwd_refs = (qb_ref, kb_ref, vb_ref, gab_ref, w2b_ref, b2b_ref, wallb_ref, mb_ref, ob_ref, st_b)

    trips = [_gla_trip_stages(t * GLA_UNROLL, n_chunks - 1 - t * GLA_UNROLL, fwd_refs, bwd_refs, head_mask)
             for t in range(n_chunks // GLA_UNROLL)]
    n_stages = 3
    for step in range(len(trips) + n_stages - 1):
        for stage in range(n_stages):
            t = step - stage
            if 0 <= t < len(trips):
                for task in trips[t][stage]:
                    task()


def _gla(p_main, p_ga, w2f, b2f, w2b, b2b, batch, seq):
    t = p_main.shape[0]
    rb = min(512, seq)
    nblk = seq // rb
    n_chunks = rb // GLA_CHUNK
    wallf, wallb, mf, mb = _gla_constants()
    wallf = jnp.asarray(np.concatenate([wallf, wallf], axis=1), BF16)
    wallb = jnp.asarray(np.concatenate([wallb, wallb], axis=1), BF16)
    mf, mb = jnp.asarray(mf, F32), jnp.asarray(mb, F32)

    fwd = lambda b, n: b * nblk + n
    bwd = lambda b, n: b * nblk + (nblk - 1 - n)

    def row_specs(rowf):
        return [
            pl.BlockSpec((rb, GLA_QK), lambda b, n: (rowf(b, n), E_GQ // GLA_QK)),
            pl.BlockSpec((rb, GLA_QK), lambda b, n: (rowf(b, n), E_GK // GLA_QK)),
            pl.BlockSpec((rb, GLA_WIDTH), lambda b, n: (rowf(b, n), E_GV // GLA_WIDTH)),
            pl.BlockSpec((rb, E_GA), lambda b, n: (rowf(b, n), 0)),
        ]

    full = lambda a: pl.BlockSpec(a.shape, lambda b, n: (0,) * a.ndim)
    consts = (w2f, b2f, w2b, b2b, wallf, wallb, mf, mb)
    return pl.pallas_call(
        functools.partial(_gla_kernel, n_chunks=n_chunks),
        grid=(batch, nblk),
        in_specs=row_specs(fwd) + row_specs(bwd) + [full(a) for a in consts],
        out_specs=[
            pl.BlockSpec((rb, GLA_WIDTH), lambda b, n: (fwd(b, n), 0)),
            pl.BlockSpec((rb, GLA_WIDTH), lambda b, n: (bwd(b, n), 0)),
        ],
        out_shape=[jax.ShapeDtypeStruct((t, GLA_WIDTH), BF16)] * 2,
        scratch_shapes=[
            pltpu.VMEM((GLA_HEADS, GLA_DK, GLA_DV), F32), pltpu.VMEM((GLA_HEADS, GLA_DK, GLA_DV), F32),
        ],
        compiler_params=_cparams(("arbitrary", "arbitrary")),
        name="gla_bidir",
    )(p_main, p_main, p_main, p_ga, p_main, p_main, p_main, p_ga, *consts)


def _rms(x, g):
    return (x * lax.rsqrt(jnp.mean(x * x, axis=-1, keepdims=True) + EPS)) * g


def _mla_prep_kernel(cq_ref, ckv_ref, kr_ref, csn_ref, gq_ref, wqt_ref, gkv_ref, wkn_ref,
                     wvt_ref, qt_ref, k_ref, vt_ref):
    scale = (MLA_NOPE + MLA_ROPE) ** -0.5 * math.log2(math.e)
    hw = MLA_HEADS * MLA_NOPE
    hr = MLA_HEADS * MLA_ROPE
    tm = cq_ref.shape[0]
    cqn =_rms(cq_ref[...].astype(F32), gq_ref[...]).astype(BF16)
    qt = lax.dot_general(wqt_ref[...], cqn, _NT, preferred_element_type=F32)
    csn = csn_ref[...]
    csn_t = csn.T
    cs2 = jnp.concatenate([csn_t[:MLA_ROPE]] * MLA_HEADS, axis=0)
    sn2 = jnp.concatenate([csn_t[MLA_ROPE:]] * MLA_HEADS, axis=0)
    qr = (qt[hw:hw + hr] * cs2 + qt[hw + hr:hw + 2 * hr] * sn2) * scale
    kvn = _rms(ckv_ref[...].astype(F32), gkv_ref[...]).astype(BF16)
    kn = _bdot(kvn, wkn_ref[...])
    vt = lax.dot_general(wvt_ref[...], kvn, _NT, preferred_element_type=F32)
    kt = kr_ref[...].astype(F32) * csn
    kr = (kt[:, :MLA_ROPE] + kt[:, MLA_ROPE:]).astype(BF16)
    npad = MLA_QK_PAD - MLA_NOPE - MLA_ROPE
    for h in range(MLA_HEADS):
        qt_ref[0, h, 0:MLA_NOPE, :] = (qt[h * MLA_NOPE:(h + 1) * MLA_NOPE] * scale).astype(BF16)
        qt_ref[0, h, MLA_NOPE:MLA_NOPE + MLA_ROPE, :] = qr[h * MLA_ROPE:(h + 1) * MLA_ROPE].astype(BF16)
        qt_ref[0, h, MLA_NOPE + MLA_ROPE:, :] = jnp.zeros((npad, tm), BF16)
        k_ref[0, h, :, 0:MLA_NOPE] = kn[:, h * MLA_NOPE:(h + 1) * MLA_NOPE].astype(BF16)
        k_ref[0, h, :, MLA_NOPE:MLA_NOPE + MLA_ROPE] = kr
        k_ref[0, h, :, MLA_NOPE + MLA_ROPE:] = jnp.zeros((tm, npad), BF16)
        vt_ref[0, h] = vt[h * MLA_V:(h + 1) * MLA_V].astype(BF16)


def _mla_prep(p_main, csn, g_q, w_uqt, g_kv, w_kn, w_vt, batch, seq):
    tm = min(EVEN_AUX_ROWS, seq)
    per_b = seq // tm
    row = lambda b, i: b * per_b + i
    full = lambda a: pl.BlockSpec(a.shape, lambda b, i: (0,) * a.ndim)
    return pl.pallas_call(
        _mla_prep_kernel,
        grid=(batch, per_b),
        in_specs=[
            pl.BlockSpec((tm, MLA_Q_RANK), lambda b, i: (row(b, i), E_CQ // MLA_Q_RANK)),
            pl.BlockSpec((tm, MLA_KV_RANK), lambda b, i: (row(b, i), E_CKV // MLA_KV_RANK)),
            pl.BlockSpec((tm, LANE), lambda b, i: (row(b, i), E_KR // LANE)),
            pl.BlockSpec((tm, LANE), lambda b, i: (row(b, i), 0)),
            full(g_q), full(w_uqt), full(g_kv), full(w_kn), full(w_vt),
        ],
        out_specs=[
            pl.BlockSpec((1, MLA_HEADS, MLA_QK_PAD, tm), lambda b, i: (b, 0, 0, i)),
            pl.BlockSpec((1, MLA_HEADS, tm, MLA_QK_PAD), lambda b, i: (b, 0, i, 0)),
            pl.BlockSpec((1, MLA_HEADS, MLA_V, tm), lambda b, i: (b, 0, 0, i)),
        ],
        out_shape=[
            jax.ShapeDtypeStruct((batch, MLA_HEADS, MLA_QK_PAD, seq), BF16),
            jax.ShapeDtypeStruct((batch, MLA_HEADS, seq, MLA_QK_PAD), BF16),
            jax.ShapeDtypeStruct((batch, MLA_HEADS, MLA_V, seq), BF16),
        ],
        compiler_params=_cparams(("arbitrary", "arbitrary")),
        name="mla_prep",
    )(p_main, p_main, p_main, csn, g_q, w_uqt, g_kv, w_kn, w_vt)


class _AttnGroup:
    def __init__(self, hh, c, qt_ref, k_ref, vt_ref, z_ref, o_ref):
        self.seq = k_ref.shape[2]
        self.tk = min(MLA_TK, self.seq)
        self.sub = min(MLA_TQ, qt_ref.shape[3])
        self.hh = hh
        self.cols = slice(c * self.sub, (c + 1) * self.sub)
        self.gate_cols = slice(hh * MLA_V, (hh + 1) * MLA_V)
        self.qt_ref, self.k_ref, self.vt_ref, self.z_ref, self.o_ref = qt_ref, k_ref, vt_ref, z_ref, o_ref
        self.nk = self.seq // self.tk
        self.s, self.m, self.acc = [], None, None

    def score_block(self, j):
        qt = self.qt_ref[0, self.hh, :, self.cols]
        s = _bdot(self.k_ref[0, self.hh, j * self.tk:(j + 1) * self.tk, :], qt)
        mj = jnp.max(s, axis=0, keepdims=True)
        self.s.append(s)
        self.m = mj if self.m is None else jnp.maximum(self.m, mj)

    def value_block(self, j):
        ones = jnp.ones((MLA_ONES_ROWS, self.tk), BF16)
        p = jnp.exp2(self.s[j] - self.m).astype(BF16)
        va = jnp.concatenate([self.vt_ref[0, self.hh, :, j * self.tk:(j + 1) * self.tk], ones], axis=0)
        oj = _bdot(va, p)
        self.acc = oj if self.acc is None else self.acc + oj
        if j == self.nk - 1:
            ot = self.acc[:MLA_V] / self.acc[MLA_V:MLA_V + 1]
            gate = _silu(self.z_ref[self.cols, self.gate_cols].astype(F32))
            self.o_ref[self.cols, self.gate_cols] = (ot.T * gate).astype(self.o_ref.dtype)

    def score_tasks(self):
        return [functools.partial(self.score_block, j) for j in range(self.nk)]

    def value_tasks(self):
        return [functools.partial(self.value_block, j) for j in range(self.nk)]


def _mla_attn_kernel(qt_ref, k_ref, vt_ref, z_ref, o_ref):
    n_cols = qt_ref.shape[3] // min(MLA_TQ, qt_ref.shape[3])
    groups = [_AttnGroup(hh, c, qt_ref, k_ref, vt_ref, z_ref, o_ref)
              for hh in range(qt_ref.shape[1]) for c in range(n_cols)]
    tasks = list(groups[0].score_tasks())
    for g in range(1, len(groups)):
        tasks += groups[g].score_tasks() + groups[g - 1].value_tasks()
    for task in tasks + groups[-1].value_tasks():
        task()


def _mla_attention(q_t, k_cat, v_t, p_main, batch, seq):
    tq = min(MLA_TQ * MLA_GROUPS, seq)
    nq = seq // tq
    hb = MLA_HEADS_PER_STEP
    return pl.pallas_call(
        _mla_attn_kernel,
        grid=(batch, MLA_HEADS // hb, nq),
        in_specs=[
            pl.BlockSpec((1, hb, MLA_QK_PAD, tq), lambda b, h, i: (b, h, 0, i)),
            pl.BlockSpec((1, hb, seq, MLA_QK_PAD), lambda b, h, i: (b, h, 0, 0)),
            pl.BlockSpec((1, hb, MLA_V, seq), lambda b, h, i: (b, h, 0, 0)),
            pl.BlockSpec((tq, hb * MLA_V), lambda b, h, i: (b * nq + i, E_MZ // (hb * MLA_V) + h)),
        ],
        out_specs=pl.BlockSpec((tq, hb * MLA_V), lambda b, h, i: (b * nq + i, h)),
        out_shape=jax.ShapeDtypeStruct((batch * seq, MLA_WIDTH), BF16),
        compiler_params=_cparams(("arbitrary", "arbitrary", "arbitrary")),
        name="mla_attention",
    )(q_t, k_cat, v_t, p_main)


def _even_out_value(x, of_ref, ob_ref, gz_ref, om_ref, mod_ref, gn_ref, w_ref, gp_ref, d):
    oa = of_ref[...].astype(F32) + ob_ref[...].astype(F32)
    gn = gn_ref[...]
    parts = [_rms(oa[:, h * GLA_DV:(h + 1) * GLA_DV], gn) for h in range(GLA_HEADS)]
    oa = jnp.concatenate(parts, axis=-1) * _silu(gz_ref[...].astype(F32))
    y = _bdot(oa.astype(BF16), w_ref[0:GLA_WIDTH, :]) + _bdot(om_ref[...], w_ref[GLA_WIDTH:, :])
    return _postnorm_residual(x, y, mod_ref, gp_ref, d)


def _even_out_kernel(x_ref, of_ref, ob_ref, gz_ref, om_ref, mod_ref, gn_ref, w_ref, gp_ref, o_ref, *, d):
    o_ref[...] = _even_out_value(x_ref[...], of_ref, ob_ref, gz_ref, om_ref, mod_ref, gn_ref, w_ref, gp_ref, d)


def _even_out_proj(x2, o_f, o_b, p_main, o_mla, mod_l, g_norm, w_out, g_post, seq):
    t, d = x2.shape
    tm = min(EVEN_AUX_ROWS, seq)
    per_b = seq // tm
    return pl.pallas_call(
        functools.partial(_even_out_kernel, d=d),
        grid=(t // tm,),
        in_specs=[
            pl.BlockSpec((tm, d), lambda i: (i, 0)),
            pl.BlockSpec((tm, GLA_WIDTH), lambda i: (i, 0)),
            pl.BlockSpec((tm, GLA_WIDTH), lambda i: (i, 0)),
            pl.BlockSpec((tm, GLA_WIDTH), lambda i: (i, E_GZ // GLA_WIDTH)),
            pl.BlockSpec((tm, MLA_WIDTH), lambda i: (i, 0)),
            pl.BlockSpec((1, 1, 3 * d), lambda i: (i // per_b, 0, 0)),
            pl.BlockSpec((1, GLA_DV), lambda i: (0, 0)),
            pl.BlockSpec((GLA_WIDTH + MLA_WIDTH, d), lambda i: (0, 0)),
            pl.BlockSpec((1, d), lambda i: (0, 0)),
        ],
        out_specs=pl.BlockSpec((tm, d), lambda i: (i, 0)),
        out_shape=jax.ShapeDtypeStruct((t, d), F32),
        compiler_params=_cparams(("arbitrary",)),
        name="even_out_proj",
    )(x2, o_f, o_b, p_main, o_mla, mod_l, g_norm, w_out, g_post)


def _odd_kernel(x_ref, mod_ref, gpre_ref, win_ref, gln_ref, bln_ref, ws_ref, bs_ref, wout_ref, gpost_ref,
                o_ref, *, d, width):
    _odd_body(x_ref[...], mod_ref, gpre_ref, win_ref, gln_ref, bln_ref, ws_ref, bs_ref, wout_ref, gpost_ref,
              o_ref, d, width)


def _even_out_odd_kernel(x_ref, of_ref, ob_ref, gz_ref, om_ref, emod_ref, gn_ref, we_ref, egp_ref,
                         mod_ref, gpre_ref, win_ref, gln_ref, bln_ref, ws_ref, bs_ref, wout_ref, gpost_ref,
                         o_ref, *, d, width):
    x = _even_out_value(x_ref[...], of_ref, ob_ref, gz_ref, om_ref, emod_ref, gn_ref, we_ref, egp_ref, d)
    _odd_body(x, mod_ref, gpre_ref, win_ref, gln_ref, bln_ref, ws_ref, bs_ref, wout_ref, gpost_ref,
              o_ref, d, width)


def _odd_body(x, mod_ref, gpre_ref, win_ref, gln_ref, bln_ref, ws_ref, bs_ref, wout_ref, gpost_ref,
              o_ref, d, width):
    tm = x.shape[0]
    gw = width // SG_GROUPS
    ncb = width // ODD_COLS
    part = min(ODD_TAIL_ROWS, tm)

    def projections(rs):
        hb = _prenorm(x[rs], mod_ref, gpre_ref, d).astype(BF16)
        proj = lambda base, c: _bdot(hb, win_ref[:, base + c * ODD_COLS:base + (c + 1) * ODD_COLS])
        vp = [proj(width, c) for c in range(ncb)]
        up, v = [], []
        for c in range(ncb):
            up.append(proj(0, c))
            v.append(_gelu(vp[c]))
        zp, u = [], []
        for c in range(ncb):
            zp.append(proj(2 * width, c))
            u.append(_gelu(up[c]))
        return jnp.concatenate(v, axis=-1), u, zp

    def tail(rs, vh, u, zp):
        mu = jnp.mean(vh, axis=-1, keepdims=True)
        vc = vh - mu
        var = jnp.mean(vc * vc, axis=-1, keepdims=True)
        vn = ((vc * lax.rsqrt(var + EPS)) * gln_ref[...] + bln_ref[...]).astype(BF16)
        rows = []
        for c in range(part // SG_CHUNK):
            cols = []
            for g in range(SG_GROUPS):
                blk = vn[c * SG_CHUNK:(c + 1) * SG_CHUNK, g * gw:(g + 1) * gw]
                cols.append(_bdot(ws_ref[g], blk) + bs_ref[:, g:g + 1])
            rows.append(jnp.concatenate(cols, axis=-1))
        mixed = jnp.concatenate(rows, axis=0) if len(rows) > 1 else rows[0]
        gated = [(u[c] * mixed[:, c * ODD_COLS:(c + 1) * ODD_COLS] * _silu(zp[c])).astype(BF16)
                 for c in range(ncb)]
        y = _bdot(jnp.concatenate(gated, axis=-1), wout_ref[...])
        o_ref[rs, :] = _postnorm_residual(x[rs], y, mod_ref, gpost_ref, d)

    parts = [slice(r0, r0 + part) for r0 in range(0, tm, part)]
    pending = None
    for rs in parts:
        cur = (rs,) + projections(rs)
        if pending is not None:
            tail(*pending)
        pending = cur
    tail(*pending)


def _odd_layer(x2, mod_l, g_pre, w_in, g_ln, b_ln, w_s, b_s_t, w_out, g_post, seq, layer):
    t, d = x2.shape
    width = w_out.shape[1]
    tm = min(ODD_ROWS, seq)
    per_b = seq // tm
    full = lambda a: pl.BlockSpec(a.shape, lambda i: (0,) * a.ndim, pipeline_mode=pl.Buffered(1))
    stacked = lambda a: pl.BlockSpec((None,) + a.shape[1:], lambda i: (layer,) + (0,) * (a.ndim - 1),
                                     pipeline_mode=pl.Buffered(1))
    return pl.pallas_call(
        functools.partial(_odd_kernel, d=d, width=width),
        grid=(t // tm,),
        in_specs=[
            pl.BlockSpec((tm, d), lambda i: (i, 0)),
            pl.BlockSpec((1, 1, 3 * d), lambda i: (i // per_b, 0, 0)),
            full(g_pre), stacked(w_in), full(g_ln), full(b_ln), stacked(w_s), full(b_s_t), stacked(w_out),
            full(g_post),
        ],
        out_specs=pl.BlockSpec((tm, d), lambda i: (i, 0)),
        out_shape=jax.ShapeDtypeStruct((t, d), F32),
        compiler_params=_cparams(("arbitrary",)),
        name="odd_sgmlp",
    )(x2, mod_l, g_pre, w_in, g_ln, b_ln, w_s, b_s_t, w_out, g_post)


def _even_out_odd_layer(x2, o_f, o_b, p_main, o_mla, emod_l, g_norm, w_out_e, eg_post,
                        mod_l, g_pre, w_in, g_ln, b_ln, w_s, b_s_t, w_out, g_post, seq, layer):
    t, d = x2.shape
    width = w_out.shape[1]
    tm = min(ODD_ROWS, seq)
    per_b = seq // tm
    full = lambda a: pl.BlockSpec(a.shape, lambda i: (0,) * a.ndim, pipeline_mode=pl.Buffered(1))
    stacked = lambda a: pl.BlockSpec((None,) + a.shape[1:], lambda i: (layer,) + (0,) * (a.ndim - 1),
                                     pipeline_mode=pl.Buffered(1))
    mod_spec = pl.BlockSpec((1, 1, 3 * d), lambda i: (i // per_b, 0, 0))
    return pl.pallas_call(
        functools.partial(_even_out_odd_kernel, d=d, width=width),
        grid=(t // tm,),
        in_specs=[
            pl.BlockSpec((tm, d), lambda i: (i, 0)),
            pl.BlockSpec((tm, GLA_WIDTH), lambda i: (i, 0)),
            pl.BlockSpec((tm, GLA_WIDTH), lambda i: (i, 0)),
            pl.BlockSpec((tm, GLA_WIDTH), lambda i: (i, E_GZ // GLA_WIDTH)),
            pl.BlockSpec((tm, MLA_WIDTH), lambda i: (i, 0)),
            mod_spec, full(g_norm), full(w_out_e), full(eg_post),
            mod_spec, full(g_pre), stacked(w_in), full(g_ln), full(b_ln), stacked(w_s), full(b_s_t),
            stacked(w_out), full(g_post),
        ],
        out_specs=pl.BlockSpec((tm, d), lambda i: (i, 0)),
        out_shape=jax.ShapeDtypeStruct((t, d), F32),
        compiler_params=_cparams(("arbitrary",)),
        name="even_out_odd",
    )(x2, o_f, o_b, p_main, o_mla, emod_l, g_norm, w_out_e, eg_post,
      mod_l, g_pre, w_in, g_ln, b_ln, w_s, b_s_t, w_out, g_post)


def _swap_halves(w):
    half = w.shape[-1] // 2
    return jnp.concatenate([w[..., half:], w[..., :half]], axis=-1)


def _even_in_weight(w):
    gq, gk, gv = w[:, 0:256], w[:, 256:512], w[:, 512:1024]
    ga_f, ga_b = w[:, 1024:1040], w[:, 1040:1056]
    gz, cq, ckv, kr, mz = w[:, 1056:1568], w[:, 1568:1824], w[:, 1824:1952], w[:, 1952:2016], w[:, 2016:2528]
    pad = jnp.zeros((w.shape[0], E_GA - 2 * GLA_GATE_RANK), w.dtype)
    return jnp.concatenate([gq, gk, gv, gz, mz, cq, ckv, kr, _swap_halves(kr), ga_f, ga_b, pad],
                           axis=-1).astype(BF16)


def _uq_weight(w):
    w3 = w.reshape(w.shape[0], MLA_HEADS, MLA_NOPE + MLA_ROPE)
    nope = w3[:, :, :MLA_NOPE].reshape(w.shape[0], -1)
    rope = w3[:, :, MLA_NOPE:]
    return jnp.concatenate([nope, rope.reshape(w.shape[0], -1), _swap_halves(rope).reshape(w.shape[0], -1)],
                           axis=-1).astype(BF16).T


def _ukv_weights(w):
    w3 = w.reshape(w.shape[0], MLA_HEADS, MLA_NOPE + MLA_V)
    w_kn = w3[:, :, :MLA_NOPE].reshape(w.shape[0], -1)
    w_v = w3[:, :, MLA_NOPE:].reshape(w.shape[0], -1)
    return w_kn.astype(BF16), w_v.astype(BF16).T


def _gate_weight(w, offset):
    out = jnp.zeros((E_GA, w.shape[1]), F32).at[offset:offset + w.shape[0]].set(w)
    hi, lo = _split_bf16(out)
    return jnp.concatenate([hi, hi, lo], axis=0)


def _rope_table(positions):
    half = MLA_ROPE // 2
    inv_freq = 1.0 / (ROPE_THETA ** (jnp.arange(0, MLA_ROPE, 2, dtype=F32) / MLA_ROPE))
    phase = jnp.concatenate([jnp.zeros((MLA_ROPE,), F32), jnp.full((half,), math.pi / 2, F32),
                             jnp.full((half,), -math.pi / 2, F32)])
    ang = positions.astype(F32).reshape(-1, 1) * jnp.tile(inv_freq, 4)
    return jnp.cos(ang + phase)


def kernel(x, c, positions, w_mod, b_mod, g_pre, g_post, w_in_e, gla_w_gate_f, gla_b_gate_f, gla_w_gate_b, gla_b_gate_b, gla_g_norm, mla_g_q, mla_w_uq, mla_g_kv, mla_w_ukv, w_out_e, w_in_o, sg_g_norm, sg_b_norm, sg_w_s, sg_b_s, w_out_o):
    batch, seq, d = x.shape
    depth = w_mod.shape[0]
    mod = _modulation(c, w_mod, b_mod)
    csn = _rope_table(positions)
    w_in_o_b, w_s_b, w_out_o_b = w_in_o.astype(BF16), sg_w_s.astype(BF16), w_out_o.astype(BF16)
    x2 = x.reshape(batch * seq, d)
    row = lambda a: a.reshape(1, -1)
    fused_odd = set()
    for l in range(depth):
        mod_l = mod[l][:, None, :]
        if l in fused_odd:
            continue
        if l % 2 == 0:
            e = l // 2
            p_main, p_ga = _even_in_proj(x2, mod_l, row(g_pre[l]), _even_in_weight(w_in_e[e]), seq)
            w_kn, w_vt = _ukv_weights(mla_w_ukv[e])
            q_t, k_cat, v_t = _mla_prep(p_main, csn, row(mla_g_q[e]), _uq_weight(mla_w_uq[e]),
                                        row(mla_g_kv[e]), w_kn, w_vt, batch, seq)
            o_f, o_b = _gla(p_main, p_ga,
                            _gate_weight(gla_w_gate_f[e], 0), row(gla_b_gate_f[e]),
                            _gate_weight(gla_w_gate_b[e], GLA_GATE_RANK), row(gla_b_gate_b[e]),
                            batch, seq)
            o_mla = _mla_attention(q_t, k_cat, v_t, p_main, batch, seq)
            if l + 1 < depth:
                o = (l + 1) // 2
                x2 = _even_out_odd_layer(
                    x2, o_f, o_b, p_main, o_mla, mod_l, row(gla_g_norm[e]), w_out_e[e].astype(BF16),
                    row(g_post[l]), mod[l + 1][:, None, :], row(g_pre[l + 1]), w_in_o_b, row(sg_g_norm[o]),
                    row(sg_b_norm[o]), w_s_b, sg_b_s[o].T, w_out_o_b, row(g_post[l + 1]), seq, o)
                fused_odd.add(l + 1)
            else:
                x2 = _even_out_proj(x2, o_f, o_b, p_main, o_mla, mod_l, row(gla_g_norm[e]),
                                    w_out_e[e].astype(BF16), row(g_post[l]), seq)
        else:
            o = l // 2
            x2 = _odd_layer(x2, mod_l, row(g_pre[l]), w_in_o_b, row(sg_g_norm[o]), row(sg_b_norm[o]),
                            w_s_b, sg_b_s[o].T, w_out_o_b, row(g_post[l]), seq, o)
    return x2.reshape(batch, seq, d)
```

```python
import functools
import math

import numpy as np
import jax
import jax.numpy as jnp
from jax import lax
from jax.experimental import pallas as pl
from jax.experimental.pallas import tpu as pltpu

F32 = jnp.float32
BF16 = jnp.bfloat16

EPS = 1e-6
GLA_HEADS = 4
GLA_DK = 64
GLA_DV = 128
GLA_GATE_RANK = 16
GLA_TAU = 16.0
GLA_CHUNK = 64
GLA_QK = GLA_HEADS * GLA_DK
GLA_WIDTH = GLA_HEADS * GLA_DV
MLA_HEADS = 4
MLA_Q_RANK = 256
MLA_KV_RANK = 128
MLA_NOPE = 128
MLA_ROPE = 64
MLA_V = 128
MLA_WIDTH = MLA_HEADS * MLA_V
MLA_QK_PAD = 256
MLA_TQ = 512
MLA_GROUPS = 8
MLA_HEADS_PER_STEP = 1
MLA_TK = 512
MLA_ONES_ROWS = 16
ROPE_THETA = 10000.0
SG_GROUPS = 8
SG_CHUNK = 128
EVEN_IN_ROWS = 1024
EVEN_IN_PART = 512
EVEN_AUX_ROWS = 1024
ODD_COLS = 512
ODD_ROWS = 512
ODD_TAIL_ROWS = 256

LANE = 128
VMEM_LIMIT = 56 * 1024 * 1024

E_GQ, E_GK, E_GV, E_GZ, E_MZ, E_CQ, E_CKV, E_KR = 0, 256, 512, 1024, 1536, 2048, 2304, 2432
E_MAIN = 2560
E_GA = 128
E_TOTAL = E_MAIN + E_GA

GLA_LEVELS = (32, 16, 8, 4, 2, 1)
GLA_UNROLL = 2


def _cparams(sem, flags=None):
    return pltpu.CompilerParams(dimension_semantics=sem, vmem_limit_bytes=VMEM_LIMIT, flags=flags)


def _silu(x):
    return x * jax.nn.sigmoid(x)


def _gelu(x):
    return 0.5 * x * (1.0 + lax.erf(x * (2.0 ** -0.5)))


def _bdot(a, b):
    return jnp.dot(a, b, preferred_element_type=F32)


def _split_bf16(x):
    hi = x.astype(BF16)
    lo = (x - hi.astype(F32)).astype(BF16)
    return hi, lo


def _mod_kernel(c_ref, w_ref, b_ref, o_ref):
    ca = _silu(c_ref[...])
    o_ref[0] = jnp.dot(ca, w_ref[0], precision=lax.Precision.HIGHEST,
                       preferred_element_type=F32) + b_ref[0]


def _modulation(c, w_mod, b_mod):
    depth, d, n = w_mod.shape
    b = c.shape[0]
    tn = 1024
    return pl.pallas_call(
        _mod_kernel,
        grid=(depth, n // tn),
        in_specs=[
            pl.BlockSpec((b, d), lambda l, j: (0, 0)),
            pl.BlockSpec((1, d, tn), lambda l, j: (l, 0, j)),
            pl.BlockSpec((1, 1, tn), lambda l, j: (l, 0, j)),
        ],
        out_specs=pl.BlockSpec((1, b, tn), lambda l, j: (l, 0, j)),
        out_shape=jax.ShapeDtypeStruct((depth, b, n), F32),
        compiler_params=_cparams(("arbitrary", "arbitrary")),
        name="adaln_mod",
    )(c, w_mod, b_mod.reshape(depth, 1, n))


def _prenorm(x, mod_ref, g_ref, d):
    ms = jnp.mean(x * x, axis=-1, keepdims=True)
    shift = mod_ref[0, :, 0:d]
    scale = mod_ref[0, :, d:2 * d]
    return (x * lax.rsqrt(ms + EPS)) * g_ref[...] * (1.0 + scale) + shift


def _postnorm_residual(x, y, mod_ref, g_ref, d):
    ms = jnp.mean(y * y, axis=-1, keepdims=True)
    gate = mod_ref[0, :, 2 * d:3 * d]
    return x + gate * ((y * lax.rsqrt(ms + EPS)) * g_ref[...])


def _even_in_kernel(x_ref, mod_ref, g_ref, w_ref, o_ref, ga_ref, *, d):
    tm = x_ref.shape[0]
    part = min(EVEN_IN_PART, tm)
    norm = lambda r0: _prenorm(x_ref[r0:r0 + part, :], mod_ref, g_ref, d).astype(BF16)
    hb_next = norm(0)
    for r0 in range(0, tm, part):
        hb = hb_next
        if r0 + part < tm:
            hb_next = norm(r0 + part)
        rs = slice(r0, r0 + part)
        for c0 in range(0, E_MAIN, 512):
            o_ref[rs, c0:c0 + 512] = _bdot(hb, w_ref[:, c0:c0 + 512]).astype(BF16)
        ga_ref[rs, :] = _bdot(hb, w_ref[:, E_MAIN:E_TOTAL])


def _even_in_proj(x2, mod_l, g_pre, w_e, seq):
    t, d = x2.shape
    tm = min(EVEN_IN_ROWS, seq)
    per_b = seq // tm
    return pl.pallas_call(
        functools.partial(_even_in_kernel, d=d),
        grid=(t // tm,),
        in_specs=[
            pl.BlockSpec((tm, d), lambda i: (i, 0)),
            pl.BlockSpec((1, 1, 3 * d), lambda i: (i // per_b, 0, 0)),
            pl.BlockSpec((1, d), lambda i: (0, 0)),
            pl.BlockSpec((d, E_TOTAL), lambda i: (0, 0)),
        ],
        out_specs=[
            pl.BlockSpec((tm, E_MAIN), lambda i: (i, 0)),
            pl.BlockSpec((tm, E_GA), lambda i: (i, 0)),
        ],
        out_shape=[
            jax.ShapeDtypeStruct((t, E_MAIN), BF16),
            jax.ShapeDtypeStruct((t, E_GA), F32),
        ],
        compiler_params=_cparams(("arbitrary",)),
        name="even_in_proj",
    )(x2, mod_l, g_pre, w_e)


def _gla_constants():
    c = GLA_CHUNK
    idx = np.arange(c)
    nl = len(GLA_LEVELS)
    wf = np.zeros((nl + 2, c, c), np.float32)
    wb = np.zeros((nl + 2, c, c), np.float32)
    mf = np.zeros((nl + 1, c, c), np.float32)
    mb = np.zeros((nl, c, c), np.float32)
    for li, s in enumerate(GLA_LEVELS):
        mid = (idx // (2 * s)) * (2 * s) + s
        upper = (idx % (2 * s)) >= s
        for t in range(c):
            if upper[t]:
                wf[li, t, mid[t]:t + 1] = 1.0
                wb[li, t, mid[t]:t] = 1.0
            else:
                wf[li, t, t + 1:mid[t]] = 1.0
                wb[li, t, t:mid[t]] = 1.0
        same = (idx[:, None] // (2 * s)) == (idx[None, :] // (2 * s))
        mf[li] = same & upper[:, None] & (~upper[None, :])
        mb[li] = same & (~upper[:, None]) & upper[None, :]
    mf[nl] = np.eye(c)
    for t in range(c):
        wf[nl, t, :t + 1] = 1.0
        wf[nl + 1, t, t + 1:] = 1.0
        wb[nl, t, t:] = 1.0
        wb[nl + 1, t, :t] = 1.0
    tile = lambda m: np.tile(m, (1, 1, GLA_HEADS))
    return (wf.reshape(-1, c), wb.reshape(-1, c), tile(mf), tile(mb))


def _gla_gate(ga, w2_ref, b2_ref):
    a_hi, a_lo = _split_bf16(ga)
    x = _bdot(jnp.concatenate([a_hi, a_lo, a_hi], axis=1), w2_ref[...]) + b2_ref[...]
    return (jnp.minimum(x, 0.0) - jnp.log(1.0 + jnp.exp(-jnp.abs(x)))) * (1.0 / GLA_TAU)


def _block_diag(blocks):
    n = len(blocks)
    zero = jnp.zeros_like(blocks[0])
    return jnp.concatenate(
        [jnp.concatenate([blocks[i] if i == j else zero for j in range(n)], axis=1) for i in range(n)],
        axis=0)


_TN = (((0,), (0,)), ((), ()))
_NT = (((1,), (1,)), ((), ()))


def _pair_dot(lhs, blocks):
    outs = []
    for p in range(len(blocks) // 2):
        outs.append(_bdot(lhs[:, p * LANE:(p + 1) * LANE], _block_diag(blocks[2 * p:2 * p + 2])))
    return jnp.concatenate(outs, axis=1)


class _GlaDir:
    def __init__(self, qs, ks, vs, gs, wall_ref, mask_ref, head_mask, forward):
        self.q, self.k, self.v, self.g = qs, ks, vs, gs
        self.wall_ref, self.mask_ref, self.head_mask, self.forward = wall_ref, mask_ref, head_mask, forward
        self.n_levels = len(GLA_LEVELS) + (1 if forward else 0)
        self.n = len(qs)

    def exponents(self):
        g2 = [jnp.concatenate(_split_bf16(g), axis=0) for g in self.g]
        f = jnp.exp(_bdot(self.wall_ref[...], jnp.concatenate(g2, axis=1)))
        self.f = [f[:, i * GLA_QK:(i + 1) * GLA_QK] for i in range(self.n)]

    def level(self, li):
        c, nl = GLA_CHUNK, len(GLA_LEVELS)
        if li >= self.n_levels:
            return
        if li == 0:
            self.att = [None] * self.n
        for i, (q, k, f) in enumerate(zip(self.q, self.k, self.f)):
            if li < nl:
                fl = f[li * c:(li + 1) * c]
                qs, ks = (q * fl).astype(BF16), (k * fl).astype(BF16)
            else:
                qs, ks = q.astype(BF16), k.astype(BF16)
            kblk = jnp.where(self.head_mask, jnp.concatenate([ks] * GLA_HEADS, axis=0), jnp.zeros((), BF16))
            r = lax.dot_general(qs, kblk, _NT, preferred_element_type=F32) * self.mask_ref[li]
            self.att[i] = r if self.att[i] is None else self.att[i] + r

    def intra_values(self):
        self.o = [_pair_dot(att.astype(BF16), [v[:, h * GLA_DV:(h + 1) * GLA_DV] for h in range(GLA_HEADS)])
                  for att, v in zip(self.att, self.v)]

    def state_terms(self):
        c, nl = GLA_CHUNK, len(GLA_LEVELS)
        self.qb = [(q * f[nl * c:(nl + 1) * c]).astype(BF16) for q, f in zip(self.q, self.f)]
        self.upd, self.dec = [], []
        tot_row = nl * c + (c - 1 if self.forward else 0)
        for k, v, f in zip(self.k, self.v, self.f):
            ke = (k * f[(nl + 1) * c:(nl + 2) * c]).astype(BF16)
            upd = []
            for p in range(GLA_HEADS // 2):
                u = lax.dot_general(ke[:, p * LANE:(p + 1) * LANE], v[:, 2 * p * GLA_DV:(2 * p + 2) * GLA_DV],
                                    _TN, preferred_element_type=F32)
                upd += [u[:GLA_DK, :GLA_DV], u[GLA_DK:, GLA_DV:]]
            self.upd.append(upd)
            self.dec.append(jnp.broadcast_to(f[tot_row:tot_row + 1], (GLA_DV, GLA_QK)).T)

    def scan(self, state_ref):
        states = [state_ref[h] for h in range(GLA_HEADS)]
        outs = []
        for i in range(self.n):
            outs.append(self.o[i] + _pair_dot(self.qb[i], [s.astype(BF16) for s in states]))
            states = [states[h] * self.dec[i][h * GLA_DK:(h + 1) * GLA_DK] + self.upd[i][h]
                      for h in range(GLA_HEADS)]
        for h in range(GLA_HEADS):
            state_ref[h] = states[h]
        return outs


def _gla_head_mask():
    hc = GLA_HEADS * GLA_CHUNK
    return (lax.broadcasted_iota(jnp.int32, (hc, GLA_QK), 0) // GLA_CHUNK
            == lax.broadcasted_iota(jnp.int32, (hc, GLA_QK), 1) // GLA_DK)


def _gla_trip_stages(first_f, first_b, fwd_refs, bwd_refs, head_mask):
    qscale = GLA_DK ** -0.5
    c, u = GLA_CHUNK, GLA_UNROLL
    plan = {
        "f": ([slice((first_f + i) * c, (first_f + i + 1) * c) for i in range(u)],
              slice(first_f * c, (first_f + u) * c), [i * c for i in range(u)], fwd_refs, True),
        "b": ([slice((first_b - i) * c, (first_b - i + 1) * c) for i in range(u)],
              slice((first_b - u + 1) * c, (first_b + 1) * c), [(u - 1 - i) * c for i in range(u)],
              bwd_refs, False),
    }
    gates, dirs = {}, {}

    def gate(name):
        _, span, _, refs, _ = plan[name]
        gates[name] = _gla_gate(refs[3][span, :], refs[4], refs[5])

    def factors(name):
        rows, _, offs, refs, forward = plan[name]
        q_ref, k_ref, v_ref = refs[:3]
        data = [(q_ref[r, :].astype(F32) * qscale, k_ref[r, :].astype(F32), v_ref[r, :],
                 gates[name][o:o + c]) for r, o in zip(rows, offs)]
        dirs[name] = _GlaDir(*zip(*data), refs[6], refs[7], head_mask, forward)
        dirs[name].exponents()

    def finish(name):
        rows, _, _, refs, _ = plan[name]
        o_ref, state_ref = refs[8:]
        for r, o in zip(rows, dirs[name].scan(state_ref)):
            o_ref[r, :] = o.astype(o_ref.dtype)

    both = lambda fn: [functools.partial(fn, "f"), functools.partial(fn, "b")]
    levels = [functools.partial(lambda n, li: dirs[n].level(li), n, li)
              for li in range(len(GLA_LEVELS) + 1) for n in ("f", "b")]
    rest = levels + [lambda: dirs["f"].intra_values(), lambda: dirs["b"].intra_values(),
                     lambda: dirs["f"].state_terms(), lambda: dirs["b"].state_terms()] + both(finish)
    return both(gate), both(factors), rest


def _gla_kernel(qf_ref, kf_ref, vf_ref, gaf_ref, qb_ref, kb_ref, vb_ref, gab_ref,
                w2f_ref, b2f_ref, w2b_ref, b2b_ref, wallf_ref, wallb_ref, mf_ref, mb_ref,
                of_ref, ob_ref, st_f, st_b, *, n_chunks):
    @pl.when(pl.program_id(1) == 0)
    def _():
        st_f[...] = jnp.zeros(st_f.shape, st_f.dtype)
        st_b[...] = jnp.zeros(st_b.shape, st_b.dtype)

    head_mask = _gla_head_mask()
    fwd_refs = (qf_ref, kf_ref, vf_ref, gaf_ref, w2f_ref, b2f_ref, wallf_ref, mf_ref, of_ref, st_f)
    bwd_refs = (qb_ref, kb_ref, vb_ref, gab_ref, w2b_ref, b2b_ref, wallb_ref, mb_ref, ob_ref, st_b)

    trips = [_gla_trip_stages(t * GLA_UNROLL, n_chunks - 1 - t * GLA_UNROLL, fwd_refs, bwd_refs, head_mask)
             for t in range(n_chunks // GLA_UNROLL)]
    n_stages = 3
    for step in range(len(trips) + n_stages - 1):
        for stage in range(n_stages):
            t = step - stage
            if 0 <= t < len(trips):
                for task in trips[t][stage]:
                    task()


def _gla(p_main, p_ga, w2f, b2f, w2b, b2b, batch, seq):
    t = p_main.shape[0]
    rb = min(512, seq)
    nblk = seq // rb
    n_chunks = rb // GLA_CHUNK
    wallf, wallb, mf, mb = _gla_constants()
    wallf = jnp.asarray(np.concatenate([wallf, wallf], axis=1), BF16)
    wallb = jnp.asarray(np.concatenate([wallb, wallb], axis=1), BF16)
    mf, mb = jnp.asarray(mf, F32), jnp.asarray(mb, F32)

    fwd = lambda b, n: b * nblk + n
    bwd = lambda b, n: b * nblk + (nblk - 1 - n)

    def row_specs(rowf):
        return [
            pl.BlockSpec((rb, GLA_QK), lambda b, n: (rowf(b, n), E_GQ // GLA_QK)),
            pl.BlockSpec((rb, GLA_QK), lambda b, n: (rowf(b, n), E_GK // GLA_QK)),
            pl.BlockSpec((rb, GLA_WIDTH), lambda b, n: (rowf(b, n), E_GV // GLA_WIDTH)),
            pl.BlockSpec((rb, E_GA), lambda b, n: (rowf(b, n), 0)),
        ]

    full = lambda a: pl.BlockSpec(a.shape, lambda b, n: (0,) * a.ndim)
    consts = (w2f, b2f, w2b, b2b, wallf, wallb, mf, mb)
    return pl.pallas_call(
        functools.partial(_gla_kernel, n_chunks=n_chunks),
        grid=(batch, nblk),
        in_specs=row_specs(fwd) + row_specs(bwd) + [full(a) for a in consts],
        out_specs=[
            pl.BlockSpec((rb, GLA_WIDTH), lambda b, n: (fwd(b, n), 0)),
            pl.BlockSpec((rb, GLA_WIDTH), lambda b, n: (bwd(b, n), 0)),
        ],
        out_shape=[jax.ShapeDtypeStruct((t, GLA_WIDTH), BF16)] * 2,
        scratch_shapes=[
            pltpu.VMEM((GLA_HEADS, GLA_DK, GLA_DV), F32), pltpu.VMEM((GLA_HEADS, GLA_DK, GLA_DV), F32),
        ],
        compiler_params=_cparams(("arbitrary", "arbitrary")),
        name="gla_bidir",
    )(p_main, p_main, p_main, p_ga, p_main, p_main, p_main, p_ga, *consts)


def _rms(x, g):
    return (x * lax.rsqrt(jnp.mean(x * x, axis=-1, keepdims=True) + EPS)) * g


def _mla_prep_kernel(cq_ref, ckv_ref, kr_ref, csn_ref, gq_ref, wqt_ref, gkv_ref, wkn_ref,
                     wvt_ref, qt_ref, k_ref, vt_ref):
    scale = (MLA_NOPE + MLA_ROPE) ** -0.5 * math.log2(math.e)
    hw = MLA_HEADS * MLA_NOPE
    hr = MLA_HEADS * MLA_ROPE
    tm = cq_ref.shape[0]
    cqn =_rms(cq_ref[...].astype(F32), gq_ref[...]).astype(BF16)
    qt = lax.dot_general(wqt_ref[...], cqn, _NT, preferred_element_type=F32)
    csn = csn_ref[...]
    csn_t = csn.T
    cs2 = jnp.concatenate([csn_t[:MLA_ROPE]] * MLA_HEADS, axis=0)
    sn2 = jnp.concatenate([csn_t[MLA_ROPE:]] * MLA_HEADS, axis=0)
    qr = (qt[hw:hw + hr] * cs2 + qt[hw + hr:hw + 2 * hr] * sn2) * scale
    kvn = _rms(ckv_ref[...].astype(F32), gkv_ref[...]).astype(BF16)
    kn = _bdot(kvn, wkn_ref[...])
    vt = lax.dot_general(wvt_ref[...], kvn, _NT, preferred_element_type=F32)
    kt = kr_ref[...].astype(F32) * csn
    kr = (kt[:, :MLA_ROPE] + kt[:, MLA_ROPE:]).astype(BF16)
    npad = MLA_QK_PAD - MLA_NOPE - MLA_ROPE
    for h in range(MLA_HEADS):
        qt_ref[0, h, 0:MLA_NOPE, :] = (qt[h * MLA_NOPE:(h + 1) * MLA_NOPE] * scale).astype(BF16)
        qt_ref[0, h, MLA_NOPE:MLA_NOPE + MLA_ROPE, :] = qr[h * MLA_ROPE:(h + 1) * MLA_ROPE].astype(BF16)
        qt_ref[0, h, MLA_NOPE + MLA_ROPE:, :] = jnp.zeros((npad, tm), BF16)
        k_ref[0, h, :, 0:MLA_NOPE] = kn[:, h * MLA_NOPE:(h + 1) * MLA_NOPE].astype(BF16)
        k_ref[0, h, :, MLA_NOPE:MLA_NOPE + MLA_ROPE] = kr
        k_ref[0, h, :, MLA_NOPE + MLA_ROPE:] = jnp.zeros((tm, npad), BF16)
        vt_ref[0, h] = vt[h * MLA_V:(h + 1) * MLA_V].astype(BF16)


def _mla_prep(p_main, csn, g_q, w_uqt, g_kv, w_kn, w_vt, batch, seq):
    tm = min(EVEN_AUX_ROWS, seq)
    per_b = seq // tm
    row = lambda b, i: b * per_b + i
    full = lambda a: pl.BlockSpec(a.shape, lambda b, i: (0,) * a.ndim)
    return pl.pallas_call(
        _mla_prep_kernel,
        grid=(batch, per_b),
        in_specs=[
            pl.BlockSpec((tm, MLA_Q_RANK), lambda b, i: (row(b, i), E_CQ // MLA_Q_RANK)),
            pl.BlockSpec((tm, MLA_KV_RANK), lambda b, i: (row(b, i), E_CKV // MLA_KV_RANK)),
            pl.BlockSpec((tm, LANE), lambda b, i: (row(b, i), E_KR // LANE)),
            pl.BlockSpec((tm, LANE), lambda b, i: (row(b, i), 0)),
            full(g_q), full(w_uqt), full(g_kv), full(w_kn), full(w_vt),
        ],
        out_specs=[
            pl.BlockSpec((1, MLA_HEADS, MLA_QK_PAD, tm), lambda b, i: (b, 0, 0, i)),
            pl.BlockSpec((1, MLA_HEADS, tm, MLA_QK_PAD), lambda b, i: (b, 0, i, 0)),
            pl.BlockSpec((1, MLA_HEADS, MLA_V, tm), lambda b, i: (b, 0, 0, i)),
        ],
        out_shape=[
            jax.ShapeDtypeStruct((batch, MLA_HEADS, MLA_QK_PAD, seq), BF16),
            jax.ShapeDtypeStruct((batch, MLA_HEADS, seq, MLA_QK_PAD), BF16),
            jax.ShapeDtypeStruct((batch, MLA_HEADS, MLA_V, seq), BF16),
        ],
        compiler_params=_cparams(("arbitrary", "arbitrary")),
        name="mla_prep",
    )(p_main, p_main, p_main, csn, g_q, w_uqt, g_kv, w_kn, w_vt)


class _AttnGroup:
    def __init__(self, hh, c, qt_ref, k_ref, vt_ref, z_ref, o_ref):
        self.seq = k_ref.shape[2]
        self.tk = min(MLA_TK, self.seq)
        self.sub = min(MLA_TQ, qt_ref.shape[3])
        self.hh = hh
        self.cols = slice(c * self.sub, (c + 1) * self.sub)
        self.gate_cols = slice(hh * MLA_V, (hh + 1) * MLA_V)
        self.qt_ref, self.k_ref, self.vt_ref, self.z_ref, self.o_ref = qt_ref, k_ref, vt_ref, z_ref, o_ref
        self.nk = self.seq // self.tk
        self.s, self.m, self.acc = [], None, None

    def score_block(self, j):
        qt = self.qt_ref[0, self.hh, :, self.cols]
        s = _bdot(self.k_ref[0, self.hh, j * self.tk:(j + 1) * self.tk, :], qt)
        mj = jnp.max(s, axis=0, keepdims=True)
        self.s.append(s)
        self.m = mj if self.m is None else jnp.maximum(self.m, mj)

    def value_block(self, j):
        ones = jnp.ones((MLA_ONES_ROWS, self.tk), BF16)
        p = jnp.exp2(self.s[j] - self.m).astype(BF16)
        va = jnp.concatenate([self.vt_ref[0, self.hh, :, j * self.tk:(j + 1) * self.tk], ones], axis=0)
        oj = _bdot(va, p)
        self.acc = oj if self.acc is None else self.acc + oj
        if j == self.nk - 1:
            ot = self.acc[:MLA_V] / self.acc[MLA_V:MLA_V + 1]
            gate = _silu(self.z_ref[self.cols, self.gate_cols].astype(F32))
            self.o_ref[self.cols, self.gate_cols] = (ot.T * gate).astype(self.o_ref.dtype)

    def score_tasks(self):
        return [functools.partial(self.score_block, j) for j in range(self.nk)]

    def value_tasks(self):
        return [functools.partial(self.value_block, j) for j in range(self.nk)]


def _mla_attn_kernel(qt_ref, k_ref, vt_ref, z_ref, o_ref):
    n_cols = qt_ref.shape[3] // min(MLA_TQ, qt_ref.shape[3])
    groups = [_AttnGroup(hh, c, qt_ref, k_ref, vt_ref, z_ref, o_ref)
              for hh in range(qt_ref.shape[1]) for c in range(n_cols)]
    tasks = list(groups[0].score_tasks())
    for g in range(1, len(groups)):
        tasks += groups[g].score_tasks() + groups[g - 1].value_tasks()
    for task in tasks + groups[-1].value_tasks():
        task()


def _mla_attention(q_t, k_cat, v_t, p_main, batch, seq):
    tq = min(MLA_TQ * MLA_GROUPS, seq)
    nq = seq // tq
    hb = MLA_HEADS_PER_STEP
    return pl.pallas_call(
        _mla_attn_kernel,
        grid=(batch, MLA_HEADS // hb, nq),
        in_specs=[
            pl.BlockSpec((1, hb, MLA_QK_PAD, tq), lambda b, h, i: (b, h, 0, i)),
            pl.BlockSpec((1, hb, seq, MLA_QK_PAD), lambda b, h, i: (b, h, 0, 0)),
            pl.BlockSpec((1, hb, MLA_V, seq), lambda b, h, i: (b, h, 0, 0)),
            pl.BlockSpec((tq, hb * MLA_V), lambda b, h, i: (b * nq + i, E_MZ // (hb * MLA_V) + h)),
        ],
        out_specs=pl.BlockSpec((tq, hb * MLA_V), lambda b, h, i: (b * nq + i, h)),
        out_shape=jax.ShapeDtypeStruct((batch * seq, MLA_WIDTH), BF16),
        compiler_params=_cparams(("arbitrary", "arbitrary", "arbitrary")),
        name="mla_attention",
    )(q_t, k_cat, v_t, p_main)


def _even_out_value(x, of_ref, ob_ref, gz_ref, om_ref, mod_ref, gn_ref, w_ref, gp_ref, d):
    oa = of_ref[...].astype(F32) + ob_ref[...].astype(F32)
    gn = gn_ref[...]
    parts = [_rms(oa[:, h * GLA_DV:(h + 1) * GLA_DV], gn) for h in range(GLA_HEADS)]
    oa = jnp.concatenate(parts, axis=-1) * _silu(gz_ref[...].astype(F32))
    y = _bdot(oa.astype(BF16), w_ref[0:GLA_WIDTH, :]) + _bdot(om_ref[...], w_ref[GLA_WIDTH:, :])
    return _postnorm_residual(x, y, mod_ref, gp_ref, d)


def _even_out_kernel(x_ref, of_ref, ob_ref, gz_ref, om_ref, mod_ref, gn_ref, w_ref, gp_ref, o_ref, *, d):
    o_ref[...] = _even_out_value(x_ref[...], of_ref, ob_ref, gz_ref, om_ref, mod_ref, gn_ref, w_ref, gp_ref, d)


def _even_out_proj(x2, o_f, o_b, p_main, o_mla, mod_l, g_norm, w_out, g_post, seq):
    t, d = x2.shape
    tm = min(EVEN_AUX_ROWS, seq)
    per_b = seq // tm
    return pl.pallas_call(
        functools.partial(_even_out_kernel, d=d),
        grid=(t // tm,),
        in_specs=[
            pl.BlockSpec((tm, d), lambda i: (i, 0)),
            pl.BlockSpec((tm, GLA_WIDTH), lambda i: (i, 0)),
            pl.BlockSpec((tm, GLA_WIDTH), lambda i: (i, 0)),
            pl.BlockSpec((tm, GLA_WIDTH), lambda i: (i, E_GZ // GLA_WIDTH)),
            pl.BlockSpec((tm, MLA_WIDTH), lambda i: (i, 0)),
            pl.BlockSpec((1, 1, 3 * d), lambda i: (i // per_b, 0, 0)),
            pl.BlockSpec((1, GLA_DV), lambda i: (0, 0)),
            pl.BlockSpec((GLA_WIDTH + MLA_WIDTH, d), lambda i: (0, 0)),
            pl.BlockSpec((1, d), lambda i: (0, 0)),
        ],
        out_specs=pl.BlockSpec((tm, d), lambda i: (i, 0)),
        out_shape=jax.ShapeDtypeStruct((t, d), F32),
        compiler_params=_cparams(("arbitrary",)),
        name="even_out_proj",
    )(x2, o_f, o_b, p_main, o_mla, mod_l, g_norm, w_out, g_post)


def _odd_kernel(x_ref, mod_ref, gpre_ref, win_ref, gln_ref, bln_ref, ws_ref, bs_ref, wout_ref, gpost_ref,
                o_ref, *, d, width):
    _odd_body(x_ref[...], mod_ref, gpre_ref, win_ref, gln_ref, bln_ref, ws_ref, bs_ref, wout_ref, gpost_ref,
              o_ref, d, width)


def _even_out_odd_kernel(x_ref, of_ref, ob_ref, gz_ref, om_ref, emod_ref, gn_ref, we_ref, egp_ref,
                         mod_ref, gpre_ref, win_ref, gln_ref, bln_ref, ws_ref, bs_ref, wout_ref, gpost_ref,
                         o_ref, *, d, width):
    x = _even_out_value(x_ref[...], of_ref, ob_ref, gz_ref, om_ref, emod_ref, gn_ref, we_ref, egp_ref, d)
    _odd_body(x, mod_ref, gpre_ref, win_ref, gln_ref, bln_ref, ws_ref, bs_ref, wout_ref, gpost_ref,
              o_ref, d, width)


def _odd_body(x, mod_ref, gpre_ref, win_ref, gln_ref, bln_ref, ws_ref, bs_ref, wout_ref, gpost_ref,
              o_ref, d, width):
    tm = x.shape[0]
    gw = width // SG_GROUPS
    ncb = width // ODD_COLS
    part = min(ODD_TAIL_ROWS, tm)

    def projections(rs):
        hb = _prenorm(x[rs], mod_ref, gpre_ref, d).astype(BF16)
        proj = lambda base, c: _bdot(hb, win_ref[:, base + c * ODD_COLS:base + (c + 1) * ODD_COLS])
        vp = [proj(width, c) for c in range(ncb)]
        up, v = [], []
        for c in range(ncb):
            up.append(proj(0, c))
            v.append(_gelu(vp[c]))
        zp, u = [], []
        for c in range(ncb):
            zp.append(proj(2 * width, c))
            u.append(_gelu(up[c]))
        return jnp.concatenate(v, axis=-1), u, zp

    def tail(rs, vh, u, zp):
        mu = jnp.mean(vh, axis=-1, keepdims=True)
        vc = vh - mu
        var = jnp.mean(vc * vc, axis=-1, keepdims=True)
        vn = ((vc * lax.rsqrt(var + EPS)) * gln_ref[...] + bln_ref[...]).astype(BF16)
        rows = []
        for c in range(part // SG_CHUNK):
            cols = []
            for g in range(SG_GROUPS):
                blk = vn[c * SG_CHUNK:(c + 1) * SG_CHUNK, g * gw:(g + 1) * gw]
                cols.append(_bdot(ws_ref[g], blk) + bs_ref[:, g:g + 1])
            rows.append(jnp.concatenate(cols, axis=-1))
        mixed = jnp.concatenate(rows, axis=0) if len(rows) > 1 else rows[0]
        gated = [(u[c] * mixed[:, c * ODD_COLS:(c + 1) * ODD_COLS] * _silu(zp[c])).astype(BF16)
                 for c in range(ncb)]
        y = _bdot(jnp.concatenate(gated, axis=-1), wout_ref[...])
        o_ref[rs, :] = _postnorm_residual(x[rs], y, mod_ref, gpost_ref, d)

    parts = [slice(r0, r0 + part) for r0 in range(0, tm, part)]
    pending = None
    for rs in parts:
        cur = (rs,) + projections(rs)
        if pending is not None:
            tail(*pending)
        pending = cur
    tail(*pending)


def _odd_layer(x2, mod_l, g_pre, w_in, g_ln, b_ln, w_s, b_s_t, w_out, g_post, seq, layer):
    t, d = x2.shape
    width = w_out.shape[1]
    tm = min(ODD_ROWS, seq)
    per_b = seq // tm
    full = lambda a: pl.BlockSpec(a.shape, lambda i: (0,) * a.ndim, pipeline_mode=pl.Buffered(1))
    stacked = lambda a: pl.BlockSpec((None,) + a.shape[1:], lambda i: (layer,) + (0,) * (a.ndim - 1),
                                     pipeline_mode=pl.Buffered(1))
    return pl.pallas_call(
        functools.partial(_odd_kernel, d=d, width=width),
        grid=(t // tm,),
        in_specs=[
            pl.BlockSpec((tm, d), lambda i: (i, 0)),
            pl.BlockSpec((1, 1, 3 * d), lambda i: (i // per_b, 0, 0)),
            full(g_pre), stacked(w_in), full(g_ln), full(b_ln), stacked(w_s), full(b_s_t), stacked(w_out),
            full(g_post),
        ],
        out_specs=pl.BlockSpec((tm, d), lambda i: (i, 0)),
        out_shape=jax.ShapeDtypeStruct((t, d), F32),
        compiler_params=_cparams(("arbitrary",)),
        name="odd_sgmlp",
    )(x2, mod_l, g_pre, w_in, g_ln, b_ln, w_s, b_s_t, w_out, g_post)


def _even_out_odd_layer(x2, o_f, o_b, p_main, o_mla, emod_l, g_norm, w_out_e, eg_post,
                        mod_l, g_pre, w_in, g_ln, b_ln, w_s, b_s_t, w_out, g_post, seq, layer):
    t, d = x2.shape
    width = w_out.shape[1]
    tm = min(ODD_ROWS, seq)
    per_b = seq // tm
    full = lambda a: pl.BlockSpec(a.shape, lambda i: (0,) * a.ndim, pipeline_mode=pl.Buffered(1))
    stacked = lambda a: pl.BlockSpec((None,) + a.shape[1:], lambda i: (layer,) + (0,) * (a.ndim - 1),
                                     pipeline_mode=pl.Buffered(1))
    mod_spec = pl.BlockSpec((1, 1, 3 * d), lambda i: (i // per_b, 0, 0))
    return pl.pallas_call(
        functools.partial(_even_out_odd_kernel, d=d, width=width),
        grid=(t // tm,),
        in_specs=[
            pl.BlockSpec((tm, d), lambda i: (i, 0)),
            pl.BlockSpec((tm, GLA_WIDTH), lambda i: (i, 0)),
            pl.BlockSpec((tm, GLA_WIDTH), lambda i: (i, 0)),
            pl.BlockSpec((tm, GLA_WIDTH), lambda i: (i, E_GZ // GLA_WIDTH)),
            pl.BlockSpec((tm, MLA_WIDTH), lambda i: (i, 0)),
            mod_spec, full(g_norm), full(w_out_e), full(eg_post),
            mod_spec, full(g_pre), stacked(w_in), full(g_ln), full(b_ln), stacked(w_s), full(b_s_t),
            stacked(w_out), full(g_post),
        ],
        out_specs=pl.BlockSpec((tm, d), lambda i: (i, 0)),
        out_shape=jax.ShapeDtypeStruct((t, d), F32),
        compiler_params=_cparams(("arbitrary",)),
        name="even_out_odd",
    )(x2, o_f, o_b, p_main, o_mla, emod_l, g_norm, w_out_e, eg_post,
      mod_l, g_pre, w_in, g_ln, b_ln, w_s, b_s_t, w_out, g_post)


def _swap_halves(w):
    half = w.shape[-1] // 2
    return jnp.concatenate([w[..., half:], w[..., :half]], axis=-1)


def _even_in_weight(w):
    gq, gk, gv = w[:, 0:256], w[:, 256:512], w[:, 512:1024]
    ga_f, ga_b = w[:, 1024:1040], w[:, 1040:1056]
    gz, cq, ckv, kr, mz = w[:, 1056:1568], w[:, 1568:1824], w[:, 1824:1952], w[:, 1952:2016], w[:, 2016:2528]
    pad = jnp.zeros((w.shape[0], E_GA - 2 * GLA_GATE_RANK), w.dtype)
    return jnp.concatenate([gq, gk, gv, gz, mz, cq, ckv, kr, _swap_halves(kr), ga_f, ga_b, pad],
                           axis=-1).astype(BF16)


def _uq_weight(w):
    w3 = w.reshape(w.shape[0], MLA_HEADS, MLA_NOPE + MLA_ROPE)
    nope = w3[:, :, :MLA_NOPE].reshape(w.shape[0], -1)
    rope = w3[:, :, MLA_NOPE:]
    return jnp.concatenate([nope, rope.reshape(w.shape[0], -1), _swap_halves(rope).reshape(w.shape[0], -1)],
                           axis=-1).astype(BF16).T


def _ukv_weights(w):
    w3 = w.reshape(w.shape[0], MLA_HEADS, MLA_NOPE + MLA_V)
    w_kn = w3[:, :, :MLA_NOPE].reshape(w.shape[0], -1)
    w_v = w3[:, :, MLA_NOPE:].reshape(w.shape[0], -1)
    return w_kn.astype(BF16), w_v.astype(BF16).T


def _gate_weight(w, offset):
    out = jnp.zeros((E_GA, w.shape[1]), F32).at[offset:offset + w.shape[0]].set(w)
    hi, lo = _split_bf16(out)
    return jnp.concatenate([hi, hi, lo], axis=0)


def _rope_table(positions):
    half = MLA_ROPE // 2
    inv_freq = 1.0 / (ROPE_THETA ** (jnp.arange(0, MLA_ROPE, 2, dtype=F32) / MLA_ROPE))
    phase = jnp.concatenate([jnp.zeros((MLA_ROPE,), F32), jnp.full((half,), math.pi / 2, F32),
                             jnp.full((half,), -math.pi / 2, F32)])
    ang = positions.astype(F32).reshape(-1, 1) * jnp.tile(inv_freq, 4)
    return jnp.cos(ang + phase)


def kernel(x, c, positions, w_mod, b_mod, g_pre, g_post, w_in_e, gla_w_gate_f, gla_b_gate_f, gla_w_gate_b, gla_b_gate_b, gla_g_norm, mla_g_q, mla_w_uq, mla_g_kv, mla_w_ukv, w_out_e, w_in_o, sg_g_norm, sg_b_norm, sg_w_s, sg_b_s, w_out_o):
    batch, seq, d = x.shape
    depth = w_mod.shape[0]
    mod = _modulation(c, w_mod, b_mod)
    csn = _rope_table(positions)
    w_in_o_b, w_s_b, w_out_o_b = w_in_o.astype(BF16), sg_w_s.astype(BF16), w_out_o.astype(BF16)
    x2 = x.reshape(batch * seq, d)
    row = lambda a: a.reshape(1, -1)
    fused_odd = set()
    for l in range(depth):
        mod_l = mod[l][:, None, :]
        if l in fused_odd:
            continue
        if l % 2 == 0:
            e = l // 2
            p_main, p_ga = _even_in_proj(x2, mod_l, row(g_pre[l]), _even_in_weight(w_in_e[e]), seq)
            w_kn, w_vt = _ukv_weights(mla_w_ukv[e])
            q_t, k_cat, v_t = _mla_prep(p_main, csn, row(mla_g_q[e]), _uq_weight(mla_w_uq[e]),
                                        row(mla_g_kv[e]), w_kn, w_vt, batch, seq)
            o_f, o_b = _gla(p_main, p_ga,
                            _gate_weight(gla_w_gate_f[e], 0), row(gla_b_gate_f[e]),
                            _gate_weight(gla_w_gate_b[e], GLA_GATE_RANK), row(gla_b_gate_b[e]),
                            batch, seq)
            o_mla = _mla_attention(q_t, k_cat, v_t, p_main, batch, seq)
            if l + 1 < depth:
                o = (l + 1) // 2
                x2 = _even_out_odd_layer(
                    x2, o_f, o_b, p_main, o_mla, mod_l, row(gla_g_norm[e]), w_out_e[e].astype(BF16),
                    row(g_post[l]), mod[l + 1][:, None, :], row(g_pre[l + 1]), w_in_o_b, row(sg_g_norm[o]),
                    row(sg_b_norm[o]), w_s_b, sg_b_s[o].T, w_out_o_b, row(g_post[l + 1]), seq, o)
                fused_odd.add(l + 1)
            else:
                x2 = _even_out_proj(x2, o_f, o_b, p_main, o_mla, mod_l, row(gla_g_norm[e]),
                                    w_out_e[e].astype(BF16), row(g_post[l]), seq)
        else:
            o = l // 2
            x2 = _odd_layer(x2, mod_l, row(g_pre[l]), w_in_o_b, row(sg_g_norm[o]), row(sg_b_norm[o]),
                            w_s_b, sg_b_s[o].T, w_out_o_b, row(g_post[l]), seq, o)
    return x2.reshape(batch, seq, d)
```

```python
import functools
import math

import numpy as np
import jax
import jax.numpy as jnp
from jax import lax
from jax.experimental import pallas as pl
from jax.experimental.pallas import tpu as pltpu

F32 = jnp.float32
BF16 = jnp.bfloat16

EPS = 1e-6
GLA_HEADS = 4
GLA_DK = 64
GLA_DV = 128
GLA_GATE_RANK = 16
GLA_TAU = 16.0
GLA_CHUNK = 64
GLA_QK = GLA_HEADS * GLA_DK
GLA_WIDTH = GLA_HEADS * GLA_DV
MLA_HEADS = 4
MLA_Q_RANK = 256
MLA_KV_RANK = 128
MLA_NOPE = 128
MLA_ROPE = 64
MLA_V = 128
MLA_WIDTH = MLA_HEADS * MLA_V
MLA_QK_PAD = 256
MLA_TQ = 512
MLA_GROUPS = 8
MLA_HEADS_PER_STEP = 1
MLA_TK = 512
MLA_ONES_ROWS = 16
ROPE_THETA = 10000.0
SG_GROUPS = 8
SG_CHUNK = 128
EVEN_IN_ROWS = 1024
EVEN_IN_PART = 512
EVEN_AUX_ROWS = 1024
ODD_COLS = 512
ODD_ROWS = 512
ODD_TAIL_ROWS = 256

LANE = 128
VMEM_LIMIT = 56 * 1024 * 1024

E_GQ, E_GK, E_GV, E_GZ, E_MZ, E_CQ, E_CKV, E_KR = 0, 256, 512, 1024, 1536, 2048, 2304, 2432
E_MAIN = 2560
E_GA = 128
E_TOTAL = E_MAIN + E_GA

GLA_LEVELS = (32, 16, 8, 4, 2, 1)
GLA_UNROLL = 2


def _cparams(sem, flags=None):
    return pltpu.CompilerParams(dimension_semantics=sem, vmem_limit_bytes=VMEM_LIMIT, flags=flags)


def _silu(x):
    return x * jax.nn.sigmoid(x)


def _gelu(x):
    return 0.5 * x * (1.0 + lax.erf(x * (2.0 ** -0.5)))


def _bdot(a, b):
    return jnp.dot(a, b, preferred_element_type=F32)


def _split_bf16(x):
    hi = x.astype(BF16)
    lo = (x - hi.astype(F32)).astype(BF16)
    return hi, lo


def _mod_kernel(c_ref, w_ref, b_ref, o_ref):
    ca = _silu(c_ref[...])
    o_ref[0] = jnp.dot(ca, w_ref[0], precision=lax.Precision.HIGHEST,
                       preferred_element_type=F32) + b_ref[0]


def _modulation(c, w_mod, b_mod):
    depth, d, n = w_mod.shape
    b = c.shape[0]
    tn = 1024
    return pl.pallas_call(
        _mod_kernel,
        grid=(depth, n // tn),
        in_specs=[
            pl.BlockSpec((b, d), lambda l, j: (0, 0)),
            pl.BlockSpec((1, d, tn), lambda l, j: (l, 0, j)),
            pl.BlockSpec((1, 1, tn), lambda l, j: (l, 0, j)),
        ],
        out_specs=pl.BlockSpec((1, b, tn), lambda l, j: (l, 0, j)),
        out_shape=jax.ShapeDtypeStruct((depth, b, n), F32),
        compiler_params=_cparams(("arbitrary", "arbitrary")),
        name="adaln_mod",
    )(c, w_mod, b_mod.reshape(depth, 1, n))


def _prenorm(x, mod_ref, g_ref, d):
    ms = jnp.mean(x * x, axis=-1, keepdims=True)
    shift = mod_ref[0, :, 0:d]
    scale = mod_ref[0, :, d:2 * d]
    return (x * lax.rsqrt(ms + EPS)) * g_ref[...] * (1.0 + scale) + shift


def _postnorm_residual(x, y, mod_ref, g_ref, d):
    ms = jnp.mean(y * y, axis=-1, keepdims=True)
    gate = mod_ref[0, :, 2 * d:3 * d]
    return x + gate * ((y * lax.rsqrt(ms + EPS)) * g_ref[...])


def _even_in_kernel(x_ref, mod_ref, g_ref, w_ref, o_ref, ga_ref, *, d):
    tm = x_ref.shape[0]
    part = min(EVEN_IN_PART, tm)
    norm = lambda r0: _prenorm(x_ref[r0:r0 + part, :], mod_ref, g_ref, d).astype(BF16)
    hb_next = norm(0)
    for r0 in range(0, tm, part):
        hb = hb_next
        if r0 + part < tm:
            hb_next = norm(r0 + part)
        rs = slice(r0, r0 + part)
        for c0 in range(0, E_MAIN, 512):
            o_ref[rs, c0:c0 + 512] = _bdot(hb, w_ref[:, c0:c0 + 512]).astype(BF16)
        ga_ref[rs, :] = _bdot(hb, w_ref[:, E_MAIN:E_TOTAL])


def _even_in_proj(x2, mod_l, g_pre, w_e, seq):
    t, d = x2.shape
    tm = min(EVEN_IN_ROWS, seq)
    per_b = seq // tm
    return pl.pallas_call(
        functools.partial(_even_in_kernel, d=d),
        grid=(t // tm,),
        in_specs=[
            pl.BlockSpec((tm, d), lambda i: (i, 0)),
            pl.BlockSpec((1, 1, 3 * d), lambda i: (i // per_b, 0, 0)),
            pl.BlockSpec((1, d), lambda i: (0, 0)),
            pl.BlockSpec((d, E_TOTAL), lambda i: (0, 0)),
        ],
        out_specs=[
            pl.BlockSpec((tm, E_MAIN), lambda i: (i, 0)),
            pl.BlockSpec((tm, E_GA), lambda i: (i, 0)),
        ],
        out_shape=[
            jax.ShapeDtypeStruct((t, E_MAIN), BF16),
            jax.ShapeDtypeStruct((t, E_GA), F32),
        ],
        compiler_params=_cparams(("arbitrary",)),
        name="even_in_proj",
    )(x2, mod_l, g_pre, w_e)


def _gla_constants():
    c = GLA_CHUNK
    idx = np.arange(c)
    nl = len(GLA_LEVELS)
    wf = np.zeros((nl + 2, c, c), np.float32)
    wb = np.zeros((nl + 2, c, c), np.float32)
    mf = np.zeros((nl + 1, c, c), np.float32)
    mb = np.zeros((nl, c, c), np.float32)
    for li, s in enumerate(GLA_LEVELS):
        mid = (idx // (2 * s)) * (2 * s) + s
        upper = (idx % (2 * s)) >= s
        for t in range(c):
            if upper[t]:
                wf[li, t, mid[t]:t + 1] = 1.0
                wb[li, t, mid[t]:t] = 1.0
            else:
                wf[li, t, t + 1:mid[t]] = 1.0
                wb[li, t, t:mid[t]] = 1.0
        same = (idx[:, None] // (2 * s)) == (idx[None, :] // (2 * s))
        mf[li] = same & upper[:, None] & (~upper[None, :])
        mb[li] = same & (~upper[:, None]) & upper[None, :]
    mf[nl] = np.eye(c)
    for t in range(c):
        wf[nl, t, :t + 1] = 1.0
        wf[nl + 1, t, t + 1:] = 1.0
        wb[nl, t, t:] = 1.0
        wb[nl + 1, t, :t] = 1.0
    tile = lambda m: np.tile(m, (1, 1, GLA_HEADS))
    return (wf.reshape(-1, c), wb.reshape(-1, c), tile(mf), tile(mb))


def _gla_gate(ga, w2_ref, b2_ref):
    a_hi, a_lo = _split_bf16(ga)
    x = _bdot(jnp.concatenate([a_hi, a_lo, a_hi], axis=1), w2_ref[...]) + b2_ref[...]
    return (jnp.minimum(x, 0.0) - jnp.log(1.0 + jnp.exp(-jnp.abs(x)))) * (1.0 / GLA_TAU)


def _block_diag(blocks):
    n = len(blocks)
    zero = jnp.zeros_like(blocks[0])
    return jnp.concatenate(
        [jnp.concatenate([blocks[i] if i == j else zero for j in range(n)], axis=1) for i in range(n)],
        axis=0)


_TN = (((0,), (0,)), ((), ()))
_NT = (((1,), (1,)), ((), ()))


def _pair_dot(lhs, blocks):
    outs = []
    for p in range(len(blocks) // 2):
        outs.append(_bdot(lhs[:, p * LANE:(p + 1) * LANE], _block_diag(blocks[2 * p:2 * p + 2])))
    return jnp.concatenate(outs, axis=1)


class _GlaDir:
    def __init__(self, qs, ks, vs, gs, wall_ref, mask_ref, head_mask, forward):
        self.q, self.k, self.v, self.g = qs, ks, vs, gs
        self.wall_ref, self.mask_ref, self.head_mask, self.forward = wall_ref, mask_ref, head_mask, forward
        self.n_levels = len(GLA_LEVELS) + (1 if forward else 0)
        self.n = len(qs)

    def exponents(self):
        g2 = [jnp.concatenate(_split_bf16(g), axis=0) for g in self.g]
        f = jnp.exp(_bdot(self.wall_ref[...], jnp.concatenate(g2, axis=1)))
        self.f = [f[:, i * GLA_QK:(i + 1) * GLA_QK] for i in range(self.n)]

    def level(self, li):
        c, nl = GLA_CHUNK, len(GLA_LEVELS)
        if li >= self.n_levels:
            return
        if li == 0:
            self.att = [None] * self.n
        for i, (q, k, f) in enumerate(zip(self.q, self.k, self.f)):
            if li < nl:
                fl = f[li * c:(li + 1) * c]
                qs, ks = (q * fl).astype(BF16), (k * fl).astype(BF16)
            else:
                qs, ks = q.astype(BF16), k.astype(BF16)
            kblk = jnp.where(self.head_mask, jnp.concatenate([ks] * GLA_HEADS, axis=0), jnp.zeros((), BF16))
            r = lax.dot_general(qs, kblk, _NT, preferred_element_type=F32) * self.mask_ref[li]
            self.att[i] = r if self.att[i] is None else self.att[i] + r

    def intra_values(self):
        self.o = [_pair_dot(att.astype(BF16), [v[:, h * GLA_DV:(h + 1) * GLA_DV] for h in range(GLA_HEADS)])
                  for att, v in zip(self.att, self.v)]

    def state_terms(self):
        c, nl = GLA_CHUNK, len(GLA_LEVELS)
        self.qb = [(q * f[nl * c:(nl + 1) * c]).astype(BF16) for q, f in zip(self.q, self.f)]
        self.upd, self.dec = [], []
        tot_row = nl * c + (c - 1 if self.forward else 0)
        for k, v, f in zip(self.k, self.v, self.f):
            ke = (k * f[(nl + 1) * c:(nl + 2) * c]).astype(BF16)
            upd = []
            for p in range(GLA_HEADS // 2):
                u = lax.dot_general(ke[:, p * LANE:(p + 1) * LANE], v[:, 2 * p * GLA_DV:(2 * p + 2) * GLA_DV],
                                    _TN, preferred_element_type=F32)
                upd += [u[:GLA_DK, :GLA_DV], u[GLA_DK:, GLA_DV:]]
            self.upd.append(upd)
            self.dec.append(jnp.broadcast_to(f[tot_row:tot_row + 1], (GLA_DV, GLA_QK)).T)

    def scan(self, state_ref):
        states = [state_ref[h] for h in range(GLA_HEADS)]
        outs = []
        for i in range(self.n):
            outs.append(self.o[i] + _pair_dot(self.qb[i], [s.astype(BF16) for s in states]))
            states = [states[h] * self.dec[i][h * GLA_DK:(h + 1) * GLA_DK] + self.upd[i][h]
                      for h in range(GLA_HEADS)]
        for h in range(GLA_HEADS):
            state_ref[h] = states[h]
        return outs


def _gla_head_mask():
    hc = GLA_HEADS * GLA_CHUNK
    return (lax.broadcasted_iota(jnp.int32, (hc, GLA_QK), 0) // GLA_CHUNK
            == lax.broadcasted_iota(jnp.int32, (hc, GLA_QK), 1) // GLA_DK)


def _gla_trip_stages(first_f, first_b, fwd_refs, bwd_refs, head_mask):
    qscale = GLA_DK ** -0.5
    c, u = GLA_CHUNK, GLA_UNROLL
    plan = {
        "f": ([slice((first_f + i) * c, (first_f + i + 1) * c) for i in range(u)],
              slice(first_f * c, (first_f + u) * c), [i * c for i in range(u)], fwd_refs, True),
        "b": ([slice((first_b - i) * c, (first_b - i + 1) * c) for i in range(u)],
              slice((first_b - u + 1) * c, (first_b + 1) * c), [(u - 1 - i) * c for i in range(u)],
              bwd_refs, False),
    }
    gates, dirs = {}, {}

    def gate(name):
        _, span, _, refs, _ = plan[name]
        gates[name] = _gla_gate(refs[3][span, :], refs[4], refs[5])

    def factors(name):
        rows, _, offs, refs, forward = plan[name]
        q_ref, k_ref, v_ref = refs[:3]
        data = [(q_ref[r, :].astype(F32) * qscale, k_ref[r, :].astype(F32), v_ref[r, :],
                 gates[name][o:o + c]) for r, o in zip(rows, offs)]
        dirs[name] = _GlaDir(*zip(*data), refs[6], refs[7], head_mask, forward)
        dirs[name].exponents()

    def finish(name):
        rows, _, _, refs, _ = plan[name]
        o_ref, state_ref = refs[8:]
        for r, o in zip(rows, dirs[name].scan(state_ref)):
            o_ref[r, :] = o.astype(o_ref.dtype)

    both = lambda fn: [functools.partial(fn, "f"), functools.partial(fn, "b")]
    levels = [functools.partial(lambda n, li: dirs[n].level(li), n, li)
              for li in range(len(GLA_LEVELS) + 1) for n in ("f", "b")]
    rest = levels + [lambda: dirs["f"].intra_values(), lambda: dirs["b"].intra_values(),
                     lambda: dirs["f"].state_terms(), lambda: dirs["b"].state_terms()] + both(finish)
    return both(gate), both(factors), rest


def _gla_kernel(qf_ref, kf_ref, vf_ref, gaf_ref, qb_ref, kb_ref, vb_ref, gab_ref,
                w2f_ref, b2f_ref, w2b_ref, b2b_ref, wallf_ref, wallb_ref, mf_ref, mb_ref,
                of_ref, ob_ref, st_f, st_b, *, n_chunks):
    @pl.when(pl.program_id(1) == 0)
    def _():
        st_f[...] = jnp.zeros(st_f.shape, st_f.dtype)
        st_b[...] = jnp.zeros(st_b.shape, st_b.dtype)

    head_mask = _gla_head_mask()
    fwd_refs = (qf_ref, kf_ref, vf_ref, gaf_ref, w2f_ref, b2f_ref, wallf_ref, mf_ref, of_ref, st_f)
    bwd_refs = (qb_ref, kb_ref, vb_ref, gab_ref, w2b_ref, b2b_ref, wallb_ref, mb_ref, ob_ref, st_b)

    trips = [_gla_trip_stages(t * GLA_UNROLL, n_chunks - 1 - t * GLA_UNROLL, fwd_refs, bwd_refs, head_mask)
             for t in range(n_chunks // GLA_UNROLL)]
    n_stages = 3
    for step in range(len(trips) + n_stages - 1):
        for stage in range(n_stages):
            t = step - stage
            if 0 <= t < len(trips):
                for task in trips[t][stage]:
                    task()


def _gla(p_main, p_ga, w2f, b2f, w2b, b2b, batch, seq):
    t = p_main.shape[0]
    rb = min(512, seq)
    nblk = seq // rb
    n_chunks = rb // GLA_CHUNK
    wallf, wallb, mf, mb = _gla_constants()
    wallf = jnp.asarray(np.concatenate([wallf, wallf], axis=1), BF16)
    wallb = jnp.asarray(np.concatenate([wallb, wallb], axis=1), BF16)
    mf, mb = jnp.asarray(mf, F32), jnp.asarray(mb, F32)

    fwd = lambda b, n: b * nblk + n
    bwd = lambda b, n: b * nblk + (nblk - 1 - n)

    def row_specs(rowf):
        return [
            pl.BlockSpec((rb, GLA_QK), lambda b, n: (rowf(b, n), E_GQ // GLA_QK)),
            pl.BlockSpec((rb, GLA_QK), lambda b, n: (rowf(b, n), E_GK // GLA_QK)),
            pl.BlockSpec((rb, GLA_WIDTH), lambda b, n: (rowf(b, n), E_GV // GLA_WIDTH)),
            pl.BlockSpec((rb, E_GA), lambda b, n: (rowf(b, n), 0)),
        ]

    full = lambda a: pl.BlockSpec(a.shape, lambda b, n: (0,) * a.ndim)
    consts = (w2f, b2f, w2b, b2b, wallf, wallb, mf, mb)
    return pl.pallas_call(
        functools.partial(_gla_kernel, n_chunks=n_chunks),
        grid=(batch, nblk),
        in_specs=row_specs(fwd) + row_specs(bwd) + [full(a) for a in consts],
        out_specs=[
            pl.BlockSpec((rb, GLA_WIDTH), lambda b, n: (fwd(b, n), 0)),
            pl.BlockSpec((rb, GLA_WIDTH), lambda b, n: (bwd(b, n), 0)),
        ],
        out_shape=[jax.ShapeDtypeStruct((t, GLA_WIDTH), BF16)] * 2,
        scratch_shapes=[
            pltpu.VMEM((GLA_HEADS, GLA_DK, GLA_DV), F32), pltpu.VMEM((GLA_HEADS, GLA_DK, GLA_DV), F32),
        ],
        compiler_params=_cparams(("arbitrary", "arbitrary")),
        name="gla_bidir",
    )(p_main, p_main, p_main, p_ga, p_main, p_main, p_main, p_ga, *consts)


def _rms(x, g):
    return (x * lax.rsqrt(jnp.mean(x * x, axis=-1, keepdims=True) + EPS)) * g


def _mla_prep_kernel(cq_ref, ckv_ref, kr_ref, csn_ref, gq_ref, wqt_ref, gkv_ref, wkn_ref,
                     wvt_ref, qt_ref, k_ref, vt_ref):
    scale = (MLA_NOPE + MLA_ROPE) ** -0.5 * math.log2(math.e)
    hw = MLA_HEADS * MLA_NOPE
    hr = MLA_HEADS * MLA_ROPE
    tm = cq_ref.shape[0]
    cqn =_rms(cq_ref[...].astype(F32), gq_ref[...]).astype(BF16)
    qt = lax.dot_general(wqt_ref[...], cqn, _NT, preferred_element_type=F32)
    csn = csn_ref[...]
    csn_t = csn.T
    cs2 = jnp.concatenate([csn_t[:MLA_ROPE]] * MLA_HEADS, axis=0)
    sn2 = jnp.concatenate([csn_t[MLA_ROPE:]] * MLA_HEADS, axis=0)
    qr = (qt[hw:hw + hr] * cs2 + qt[hw + hr:hw + 2 * hr] * sn2) * scale
    kvn = _rms(ckv_ref[...].astype(F32), gkv_ref[...]).astype(BF16)
    kn = _bdot(kvn, wkn_ref[...])
    vt = lax.dot_general(wvt_ref[...], kvn, _NT, preferred_element_type=F32)
    kt = kr_ref[...].astype(F32) * csn
    kr = (kt[:, :MLA_ROPE] + kt[:, MLA_ROPE:]).astype(BF16)
    npad = MLA_QK_PAD - MLA_NOPE - MLA_ROPE
    for h in range(MLA_HEADS):
        qt_ref[0, h, 0:MLA_NOPE, :] = (qt[h * MLA_NOPE:(h + 1) * MLA_NOPE] * scale).astype(BF16)
        qt_ref[0, h, MLA_NOPE:MLA_NOPE + MLA_ROPE, :] = qr[h * MLA_ROPE:(h + 1) * MLA_ROPE].astype(BF16)
        qt_ref[0, h, MLA_NOPE + MLA_ROPE:, :] = jnp.zeros((npad, tm), BF16)
        k_ref[0, h, :, 0:MLA_NOPE] = kn[:, h * MLA_NOPE:(h + 1) * MLA_NOPE].astype(BF16)
        k_ref[0, h, :, MLA_NOPE:MLA_NOPE + MLA_ROPE] = kr
        k_ref[0, h, :, MLA_NOPE + MLA_ROPE:] = jnp.zeros((tm, npad), BF16)
        vt_ref[0, h] = vt[h * MLA_V:(h + 1) * MLA_V].astype(BF16)


def _mla_prep(p_main, csn, g_q, w_uqt, g_kv, w_kn, w_vt, batch, seq):
    tm = min(EVEN_AUX_ROWS, seq)
    per_b = seq // tm
    row = lambda b, i: b * per_b + i
    full = lambda a: pl.BlockSpec(a.shape, lambda b, i: (0,) * a.ndim)
    return pl.pallas_call(
        _mla_prep_kernel,
        grid=(batch, per_b),
        in_specs=[
            pl.BlockSpec((tm, MLA_Q_RANK), lambda b, i: (row(b, i), E_CQ // MLA_Q_RANK)),
            pl.BlockSpec((tm, MLA_KV_RANK), lambda b, i: (row(b, i), E_CKV // MLA_KV_RANK)),
            pl.BlockSpec((tm, LANE), lambda b, i: (row(b, i), E_KR // LANE)),
            pl.BlockSpec((tm, LANE), lambda b, i: (row(b, i), 0)),
            full(g_q), full(w_uqt), full(g_kv), full(w_kn), full(w_vt),
        ],
        out_specs=[
            pl.BlockSpec((1, MLA_HEADS, MLA_QK_PAD, tm), lambda b, i: (b, 0, 0, i)),
            pl.BlockSpec((1, MLA_HEADS, tm, MLA_QK_PAD), lambda b, i: (b, 0, i, 0)),
            pl.BlockSpec((1, MLA_HEADS, MLA_V, tm), lambda b, i: (b, 0, 0, i)),
        ],
        out_shape=[
            jax.ShapeDtypeStruct((batch, MLA_HEADS, MLA_QK_PAD, seq), BF16),
            jax.ShapeDtypeStruct((batch, MLA_HEADS, seq, MLA_QK_PAD), BF16),
            jax.ShapeDtypeStruct((batch, MLA_HEADS, MLA_V, seq), BF16),
        ],
        compiler_params=_cparams(("arbitrary", "arbitrary")),
        name="mla_prep",
    )(p_main, p_main, p_main, csn, g_q, w_uqt, g_kv, w_kn, w_vt)


def _even_in_prep_kernel(x_ref, mod_ref, g_ref, w_ref, csn_ref, gq_ref, wqt_ref, gkv_ref, wkn_ref, wvt_ref,
                         o_ref, ga_ref, qt_ref, k_ref, vt_ref, *, d):
    _even_in_kernel(x_ref, mod_ref, g_ref, w_ref, o_ref, ga_ref, d=d)
    _mla_prep_kernel(o_ref.at[:, E_CQ:E_CQ + MLA_Q_RANK], o_ref.at[:, E_CKV:E_CKV + MLA_KV_RANK],
                     o_ref.at[:, E_KR:E_KR + LANE], csn_ref, gq_ref, wqt_ref, gkv_ref, wkn_ref, wvt_ref,
                     qt_ref, k_ref, vt_ref)


def _even_in_prep(x2, mod_l, g_pre, w_e, csn, g_q, w_uqt, g_kv, w_kn, w_vt, batch, seq):
    t, d = x2.shape
    tm = min(EVEN_IN_ROWS, seq)
    per_b = seq // tm
    full = lambda a: pl.BlockSpec(a.shape, lambda i: (0,) * a.ndim)
    return pl.pallas_call(
        functools.partial(_even_in_prep_kernel, d=d),
        grid=(t // tm,),
        in_specs=[
            pl.BlockSpec((tm, d), lambda i: (i, 0)),
            pl.BlockSpec((1, 1, 3 * d), lambda i: (i // per_b, 0, 0)),
            pl.BlockSpec((1, d), lambda i: (0, 0)),
            pl.BlockSpec((d, E_TOTAL), lambda i: (0, 0)),
            pl.BlockSpec((tm, LANE), lambda i: (i, 0)),
            full(g_q), full(w_uqt), full(g_kv), full(w_kn), full(w_vt),
        ],
        out_specs=[
            pl.BlockSpec((tm, E_MAIN), lambda i: (i, 0)),
            pl.BlockSpec((tm, E_GA), lambda i: (i, 0)),
            pl.BlockSpec((1, MLA_HEADS, MLA_QK_PAD, tm), lambda i: (i // per_b, 0, 0, i % per_b)),
            pl.BlockSpec((1, MLA_HEADS, tm, MLA_QK_PAD), lambda i: (i // per_b, 0, i % per_b, 0)),
            pl.BlockSpec((1, MLA_HEADS, MLA_V, tm), lambda i: (i // per_b, 0, 0, i % per_b)),
        ],
        out_shape=[
            jax.ShapeDtypeStruct((t, E_MAIN), BF16),
            jax.ShapeDtypeStruct((t, E_GA), F32),
            jax.ShapeDtypeStruct((batch, MLA_HEADS, MLA_QK_PAD, seq), BF16),
            jax.ShapeDtypeStruct((batch, MLA_HEADS, seq, MLA_QK_PAD), BF16),
            jax.ShapeDtypeStruct((batch, MLA_HEADS, MLA_V, seq), BF16),
        ],
        compiler_params=_cparams(("arbitrary",)),
        name="even_in_prep",
    )(x2, mod_l, g_pre, w_e, csn, g_q, w_uqt, g_kv, w_kn, w_vt)


class _AttnGroup:
    def __init__(self, hh, c, qt_ref, k_ref, vt_ref, z_ref, o_ref):
        self.seq = k_ref.shape[2]
        self.tk = min(MLA_TK, self.seq)
        self.sub = min(MLA_TQ, qt_ref.shape[3])
        self.hh = hh
        self.cols = slice(c * self.sub, (c + 1) * self.sub)
        self.gate_cols = slice(hh * MLA_V, (hh + 1) * MLA_V)
        self.qt_ref, self.k_ref, self.vt_ref, self.z_ref, self.o_ref = qt_ref, k_ref, vt_ref, z_ref, o_ref
        self.nk = self.seq // self.tk
        self.s, self.m, self.acc = [], None, None

    def score_block(self, j):
        qt = self.qt_ref[0, self.hh, :, self.cols]
        s = _bdot(self.k_ref[0, self.hh, j * self.tk:(j + 1) * self.tk, :], qt)
        mj = jnp.max(s, axis=0, keepdims=True)
        self.s.append(s)
        self.m = mj if self.m is None else jnp.maximum(self.m, mj)

    def value_block(self, j):
        ones = jnp.ones((MLA_ONES_ROWS, self.tk), BF16)
        p = jnp.exp2(self.s[j] - self.m).astype(BF16)
        va = jnp.concatenate([self.vt_ref[0, self.hh, :, j * self.tk:(j + 1) * self.tk], ones], axis=0)
        oj = _bdot(va, p)
        self.acc = oj if self.acc is None else self.acc + oj
        if j == self.nk - 1:
            ot = self.acc[:MLA_V] / self.acc[MLA_V:MLA_V + 1]
            gate = _silu(self.z_ref[self.cols, self.gate_cols].astype(F32))
            self.o_ref[self.cols, self.gate_cols] = (ot.T * gate).astype(self.o_ref.dtype)

    def score_tasks(self):
        return [functools.partial(self.score_block, j) for j in range(self.nk)]

    def value_tasks(self):
        return [functools.partial(self.value_block, j) for j in range(self.nk)]


def _mla_attn_kernel(qt_ref, k_ref, vt_ref, z_ref, o_ref):
    n_cols = qt_ref.shape[3] // min(MLA_TQ, qt_ref.shape[3])
    groups = [_AttnGroup(hh, c, qt_ref, k_ref, vt_ref, z_ref, o_ref)
              for hh in range(qt_ref.shape[1]) for c in range(n_cols)]
    tasks = list(groups[0].score_tasks())
    for g in range(1, len(groups)):
        tasks += groups[g].score_tasks() + groups[g - 1].value_tasks()
    for task in tasks + groups[-1].value_tasks():
        task()


def _mla_attention(q_t, k_cat, v_t, p_main, batch, seq):
    tq = min(MLA_TQ * MLA_GROUPS, seq)
    nq = seq // tq
    hb = MLA_HEADS_PER_STEP
    return pl.pallas_call(
        _mla_attn_kernel,
        grid=(batch, MLA_HEADS // hb, nq),
        in_specs=[
            pl.BlockSpec((1, hb, MLA_QK_PAD, tq), lambda b, h, i: (b, h, 0, i)),
            pl.BlockSpec((1, hb, seq, MLA_QK_PAD), lambda b, h, i: (b, h, 0, 0)),
            pl.BlockSpec((1, hb, MLA_V, seq), lambda b, h, i: (b, h, 0, 0)),
            pl.BlockSpec((tq, hb * MLA_V), lambda b, h, i: (b * nq + i, E_MZ // (hb * MLA_V) + h)),
        ],
        out_specs=pl.BlockSpec((tq, hb * MLA_V), lambda b, h, i: (b * nq + i, h)),
        out_shape=jax.ShapeDtypeStruct((batch * seq, MLA_WIDTH), BF16),
        compiler_params=_cparams(("arbitrary", "arbitrary", "arbitrary")),
        name="mla_attention",
    )(q_t, k_cat, v_t, p_main)


def _even_out_value(x, of_ref, ob_ref, gz_ref, om_ref, mod_ref, gn_ref, w_ref, gp_ref, d):
    oa = of_ref[...].astype(F32) + ob_ref[...].astype(F32)
    gn = gn_ref[...]
    parts = [_rms(oa[:, h * GLA_DV:(h + 1) * GLA_DV], gn) for h in range(GLA_HEADS)]
    oa = jnp.concatenate(parts, axis=-1) * _silu(gz_ref[...].astype(F32))
    y = _bdot(oa.astype(BF16), w_ref[0:GLA_WIDTH, :]) + _bdot(om_ref[...], w_ref[GLA_WIDTH:, :])
    return _postnorm_residual(x, y, mod_ref, gp_ref, d)


def _even_out_kernel(x_ref, of_ref, ob_ref, gz_ref, om_ref, mod_ref, gn_ref, w_ref, gp_ref, o_ref, *, d):
    o_ref[...] = _even_out_value(x_ref[...], of_ref, ob_ref, gz_ref, om_ref, mod_ref, gn_ref, w_ref, gp_ref, d)


def _even_out_proj(x2, o_f, o_b, p_main, o_mla, mod_l, g_norm, w_out, g_post, seq):
    t, d = x2.shape
    tm = min(EVEN_AUX_ROWS, seq)
    per_b = seq // tm
    return pl.pallas_call(
        functools.partial(_even_out_kernel, d=d),
        grid=(t // tm,),
        in_specs=[
            pl.BlockSpec((tm, d), lambda i: (i, 0)),
            pl.BlockSpec((tm, GLA_WIDTH), lambda i: (i, 0)),
            pl.BlockSpec((tm, GLA_WIDTH), lambda i: (i, 0)),
            pl.BlockSpec((tm, GLA_WIDTH), lambda i: (i, E_GZ // GLA_WIDTH)),
            pl.BlockSpec((tm, MLA_WIDTH), lambda i: (i, 0)),
            pl.BlockSpec((1, 1, 3 * d), lambda i: (i // per_b, 0, 0)),
            pl.BlockSpec((1, GLA_DV), lambda i: (0, 0)),
            pl.BlockSpec((GLA_WIDTH + MLA_WIDTH, d), lambda i: (0, 0)),
            pl.BlockSpec((1, d), lambda i: (0, 0)),
        ],
        out_specs=pl.BlockSpec((tm, d), lambda i: (i, 0)),
        out_shape=jax.ShapeDtypeStruct((t, d), F32),
        compiler_params=_cparams(("arbitrary",)),
        name="even_out_proj",
    )(x2, o_f, o_b, p_main, o_mla, mod_l, g_norm, w_out, g_post)


def _odd_kernel(x_ref, mod_ref, gpre_ref, win_ref, gln_ref, bln_ref, ws_ref, bs_ref, wout_ref, gpost_ref,
                o_ref, *, d, width):
    _odd_body(x_ref[...], mod_ref, gpre_ref, win_ref, gln_ref, bln_ref, ws_ref, bs_ref, wout_ref, gpost_ref,
              o_ref, d, width)


def _even_out_odd_kernel(x_ref, of_ref, ob_ref, gz_ref, om_ref, emod_ref, gn_ref, we_ref, egp_ref,
                         mod_ref, gpre_ref, win_ref, gln_ref, bln_ref, ws_ref, bs_ref, wout_ref, gpost_ref,
                         o_ref, *, d, width):
    x = _even_out_value(x_ref[...], of_ref, ob_ref, gz_ref, om_ref, emod_ref, gn_ref, we_ref, egp_ref, d)
    _odd_body(x, mod_ref, gpre_ref, win_ref, gln_ref, bln_ref, ws_ref, bs_ref, wout_ref, gpost_ref,
              o_ref, d, width)


def _odd_body(x, mod_ref, gpre_ref, win_ref, gln_ref, bln_ref, ws_ref, bs_ref, wout_ref, gpost_ref,
              o_ref, d, width):
    tm = x.shape[0]
    gw = width // SG_GROUPS
    ncb = width // ODD_COLS
    part = min(ODD_TAIL_ROWS, tm)

    def projections(rs):
        hb = _prenorm(x[rs], mod_ref, gpre_ref, d).astype(BF16)
        proj = lambda base, c: _bdot(hb, win_ref[:, base + c * ODD_COLS:base + (c + 1) * ODD_COLS])
        vp = [proj(width, c) for c in range(ncb)]
        up, v = [], []
        for c in range(ncb):
            up.append(proj(0, c))
            v.append(_gelu(vp[c]))
        zp, u = [], []
        for c in range(ncb):
            zp.append(proj(2 * width, c))
            u.append(_gelu(up[c]))
        return jnp.concatenate(v, axis=-1), u, zp

    def tail(rs, vh, u, zp):
        mu = jnp.mean(vh, axis=-1, keepdims=True)
        vc = vh - mu
        var = jnp.mean(vc * vc, axis=-1, keepdims=True)
        vn = ((vc * lax.rsqrt(var + EPS)) * gln_ref[...] + bln_ref[...]).astype(BF16)
        rows = []
        for c in range(part // SG_CHUNK):
            cols = []
            for g in range(SG_GROUPS):
                blk = vn[c * SG_CHUNK:(c + 1) * SG_CHUNK, g * gw:(g + 1) * gw]
                cols.append(_bdot(ws_ref[g], blk) + bs_ref[:, g:g + 1])
            rows.append(jnp.concatenate(cols, axis=-1))
        mixed = jnp.concatenate(rows, axis=0) if len(rows) > 1 else rows[0]
        gated = [(u[c] * mixed[:, c * ODD_COLS:(c + 1) * ODD_COLS] * _silu(zp[c])).astype(BF16)
                 for c in range(ncb)]
        y = _bdot(jnp.concatenate(gated, axis=-1), wout_ref[...])
        o_ref[rs, :] = _postnorm_residual(x[rs], y, mod_ref, gpost_ref, d)

    parts = [slice(r0, r0 + part) for r0 in range(0, tm, part)]
    pending = None
    for rs in parts:
        cur = (rs,) + projections(rs)
        if pending is not None:
            tail(*pending)
        pending = cur
    tail(*pending)


def _odd_layer(x2, mod_l, g_pre, w_in, g_ln, b_ln, w_s, b_s_t, w_out, g_post, seq, layer):
    t, d = x2.shape
    width = w_out.shape[1]
    tm = min(ODD_ROWS, seq)
    per_b = seq // tm
    full = lambda a: pl.BlockSpec(a.shape, lambda i: (0,) * a.ndim, pipeline_mode=pl.Buffered(1))
    stacked = lambda a: pl.BlockSpec((None,) + a.shape[1:], lambda i: (layer,) + (0,) * (a.ndim - 1),
                                     pipeline_mode=pl.Buffered(1))
    return pl.pallas_call(
        functools.partial(_odd_kernel, d=d, width=width),
        grid=(t // tm,),
        in_specs=[
            pl.BlockSpec((tm, d), lambda i: (i, 0)),
            pl.BlockSpec((1, 1, 3 * d), lambda i: (i // per_b, 0, 0)),
            full(g_pre), stacked(w_in), full(g_ln), full(b_ln), stacked(w_s), full(b_s_t), stacked(w_out),
            full(g_post),
        ],
        out_specs=pl.BlockSpec((tm, d), lambda i: (i, 0)),
        out_shape=jax.ShapeDtypeStruct((t, d), F32),
        compiler_params=_cparams(("arbitrary",)),
        name="odd_sgmlp",
    )(x2, mod_l, g_pre, w_in, g_ln, b_ln, w_s, b_s_t, w_out, g_post)


def _even_out_odd_layer(x2, o_f, o_b, p_main, o_mla, emod_l, g_norm, w_out_e, eg_post,
                        mod_l, g_pre, w_in, g_ln, b_ln, w_s, b_s_t, w_out, g_post, seq, layer):
    t, d = x2.shape
    width = w_out.shape[1]
    tm = min(ODD_ROWS, seq)
    per_b = seq // tm
    full = lambda a: pl.BlockSpec(a.shape, lambda i: (0,) * a.ndim, pipeline_mode=pl.Buffered(1))
    stacked = lambda a: pl.BlockSpec((None,) + a.shape[1:], lambda i: (layer,) + (0,) * (a.ndim - 1),
                                     pipeline_mode=pl.Buffered(1))
    mod_spec = pl.BlockSpec((1, 1, 3 * d), lambda i: (i // per_b, 0, 0))
    return pl.pallas_call(
        functools.partial(_even_out_odd_kernel, d=d, width=width),
        grid=(t // tm,),
        in_specs=[
            pl.BlockSpec((tm, d), lambda i: (i, 0)),
            pl.BlockSpec((tm, GLA_WIDTH), lambda i: (i, 0)),
            pl.BlockSpec((tm, GLA_WIDTH), lambda i: (i, 0)),
            pl.BlockSpec((tm, GLA_WIDTH), lambda i: (i, E_GZ // GLA_WIDTH)),
            pl.BlockSpec((tm, MLA_WIDTH), lambda i: (i, 0)),
            mod_spec, full(g_norm), full(w_out_e), full(eg_post),
            mod_spec, full(g_pre), stacked(w_in), full(g_ln), full(b_ln), stacked(w_s), full(b_s_t),
            stacked(w_out), full(g_post),
        ],
        out_specs=pl.BlockSpec((tm, d), lambda i: (i, 0)),
        out_shape=jax.ShapeDtypeStruct((t, d), F32),
        compiler_params=_cparams(("arbitrary",)),
        name="even_out_odd",
    )(x2, o_f, o_b, p_main, o_mla, emod_l, g_norm, w_out_e, eg_post,
      mod_l, g_pre, w_in, g_ln, b_ln, w_s, b_s_t, w_out, g_post)


def _swap_halves(w):
    half = w.shape[-1] // 2
    return jnp.concatenate([w[..., half:], w[..., :half]], axis=-1)


def _even_in_weight(w):
    gq, gk, gv = w[:, 0:256], w[:, 256:512], w[:, 512:1024]
    ga_f, ga_b = w[:, 1024:1040], w[:, 1040:1056]
    gz, cq, ckv, kr, mz = w[:, 1056:1568], w[:, 1568:1824], w[:, 1824:1952], w[:, 1952:2016], w[:, 2016:2528]
    pad = jnp.zeros((w.shape[0], E_GA - 2 * GLA_GATE_RANK), w.dtype)
    return jnp.concatenate([gq, gk, gv, gz, mz, cq, ckv, kr, _swap_halves(kr), ga_f, ga_b, pad],
                           axis=-1).astype(BF16)


def _uq_weight(w):
    w3 = w.reshape(w.shape[0], MLA_HEADS, MLA_NOPE + MLA_ROPE)
    nope = w3[:, :, :MLA_NOPE].reshape(w.shape[0], -1)
    rope = w3[:, :, MLA_NOPE:]
    return jnp.concatenate([nope, rope.reshape(w.shape[0], -1), _swap_halves(rope).reshape(w.shape[0], -1)],
                           axis=-1).astype(BF16).T


def _ukv_weights(w):
    w3 = w.reshape(w.shape[0], MLA_HEADS, MLA_NOPE + MLA_V)
    w_kn = w3[:, :, :MLA_NOPE].reshape(w.shape[0], -1)
    w_v = w3[:, :, MLA_NOPE:].reshape(w.shape[0], -1)
    return w_kn.astype(BF16), w_v.astype(BF16).T


def _gate_weight(w, offset):
    out = jnp.zeros((E_GA, w.shape[1]), F32).at[offset:offset + w.shape[0]].set(w)
    hi, lo = _split_bf16(out)
    return jnp.concatenate([hi, hi, lo], axis=0)


def _rope_table(positions):
    half = MLA_ROPE // 2
    inv_freq = 1.0 / (ROPE_THETA ** (jnp.arange(0, MLA_ROPE, 2, dtype=F32) / MLA_ROPE))
    phase = jnp.concatenate([jnp.zeros((MLA_ROPE,), F32), jnp.full((half,), math.pi / 2, F32),
                             jnp.full((half,), -math.pi / 2, F32)])
    ang = positions.astype(F32).reshape(-1, 1) * jnp.tile(inv_freq, 4)
    return jnp.cos(ang + phase)


def kernel(x, c, positions, w_mod, b_mod, g_pre, g_post, w_in_e, gla_w_gate_f, gla_b_gate_f, gla_w_gate_b, gla_b_gate_b, gla_g_norm, mla_g_q, mla_w_uq, mla_g_kv, mla_w_ukv, w_out_e, w_in_o, sg_g_norm, sg_b_norm, sg_w_s, sg_b_s, w_out_o):
    batch, seq, d = x.shape
    depth = w_mod.shape[0]
    mod = _modulation(c, w_mod, b_mod)
    csn = _rope_table(positions)
    w_in_o_b, w_s_b, w_out_o_b = w_in_o.astype(BF16), sg_w_s.astype(BF16), w_out_o.astype(BF16)
    x2 = x.reshape(batch * seq, d)
    row = lambda a: a.reshape(1, -1)
    fused_odd = set()
    for l in range(depth):
        mod_l = mod[l][:, None, :]
        if l in fused_odd:
            continue
        if l % 2 == 0:
            e = l // 2
            w_kn, w_vt = _ukv_weights(mla_w_ukv[e])
            p_main, p_ga, q_t, k_cat, v_t = _even_in_prep(
                x2, mod_l, row(g_pre[l]), _even_in_weight(w_in_e[e]), csn, row(mla_g_q[e]),
                _uq_weight(mla_w_uq[e]), row(mla_g_kv[e]), w_kn, w_vt, batch, seq)
            o_f, o_b = _gla(p_main, p_ga,
                            _gate_weight(gla_w_gate_f[e], 0), row(gla_b_gate_f[e]),
                            _gate_weight(gla_w_gate_b[e], GLA_GATE_RANK), row(gla_b_gate_b[e]),
                            batch, seq)
            o_mla = _mla_attention(q_t, k_cat, v_t, p_main, batch, seq)
            if l + 1 < depth:
                o = (l + 1) // 2
                x2 = _even_out_odd_layer(
                    x2, o_f, o_b, p_main, o_mla, mod_l, row(gla_g_norm[e]), w_out_e[e].astype(BF16),
                    row(g_post[l]), mod[l + 1][:, None, :], row(g_pre[l + 1]), w_in_o_b, row(sg_g_norm[o]),
                    row(sg_b_norm[o]), w_s_b, sg_b_s[o].T, w_out_o_b, row(g_post[l + 1]), seq, o)
                fused_odd.add(l + 1)
            else:
                x2 = _even_out_proj(x2, o_f, o_b, p_main, o_mla, mod_l, row(gla_g_norm[e]),
                                    w_out_e[e].astype(BF16), row(g_post[l]), seq)
        else:
            o = l // 2
            x2 = _odd_layer(x2, mod_l, row(g_pre[l]), w_in_o_b, row(sg_g_norm[o]), row(sg_b_norm[o]),
                            w_s_b, sg_b_s[o].T, w_out_o_b, row(g_post[l]), seq, o)
    return x2.reshape(batch, seq, d)
```
